```python
import math
import jax, jax.numpy as jnp
from jax import lax
import numpy as np

D_MODEL = 1024
BATCH = 4
SEQ = 8192
DEPTH = 1
DEC_BATCH = 8
DEC_SEQ = 2048
PAST_LEN = 128

N_HEADS = 8
HEAD_DIM = 64
QK_W = N_HEADS * 2 * HEAD_DIM
V_W = N_HEADS * 2 * HEAD_DIM
Q_BLOCK = 128
ROPE_THETA = 10000.0
POOL_WINDOWS = (2, 4, 8, 16)
N_POOL_GROUPS = 4
D_POOL = 1024
POOL_GC = D_POOL // N_POOL_GROUPS
IN_COLS = D_POOL + 2 * QK_W + V_W + 2 * D_MODEL
N_EXPERTS = 32
TOP_K = 4
D_FF = 1024
SWIGLU_LIMIT = 7.0
SWIGLU_ALPHA = 1.702
EXPERT_BLOCK = 128
LN_EPS = 1e-5
DEEPNORM_ALPHA = (2.0 * DEPTH) ** 0.25
DEEPNORM_BETA = (8.0 * DEPTH) ** -0.25

kernel_name = "hybrid_pool_diffattn_moe_encoder"


def _layernorm_plain(x):
    xf = x.astype(jnp.float32)
    mu = jnp.mean(xf, axis=-1, keepdims=True)
    var = jnp.mean(jnp.square(xf - mu), axis=-1, keepdims=True)
    return ((xf - mu) * lax.rsqrt(var + LN_EPS)).astype(x.dtype)


def _layernorm(x, g, b):
    xf = x.astype(jnp.float32)
    mu = jnp.mean(xf, axis=-1, keepdims=True)
    var = jnp.mean(jnp.square(xf - mu), axis=-1, keepdims=True)
    y = (xf - mu) * lax.rsqrt(var + LN_EPS) * g.astype(jnp.float32) + b.astype(jnp.float32)
    return y.astype(x.dtype)


def _modulate(x, shift, scale):
    return x * (1 + scale[:, None, :]) + shift[:, None, :]


def _rope(x, seq_len):
    inv_freq = ROPE_THETA ** (-jnp.arange(0, HEAD_DIM, 2, dtype=jnp.float32) / HEAD_DIM)
    ang = jnp.arange(seq_len, dtype=jnp.float32)[:, None] * inv_freq[None, :]
    cos = jnp.cos(ang).astype(x.dtype)[None, :, None, None, :]
    sin = jnp.sin(ang).astype(x.dtype)[None, :, None, None, :]
    x1, x2 = x[..., : HEAD_DIM // 2], x[..., HEAD_DIM // 2:]
    return jnp.concatenate([x1 * cos - x2 * sin, x2 * cos + x1 * sin], axis=-1)


def _multiscale_pool(ua, pool_w, pool_scale):
    B, S, _ = ua.shape
    g = ua.reshape(B, S, N_POOL_GROUPS, POOL_GC).astype(jnp.float32)
    cs = jnp.concatenate([jnp.zeros((B, 1, N_POOL_GROUPS, POOL_GC), jnp.float32),
                          jnp.cumsum(g, axis=1)], axis=1)
    pos = jnp.arange(S)
    outs = []
    for gi, w in enumerate(POOL_WINDOWS):
        lo = jnp.maximum(pos - w // 2, 0)
        hi = jnp.minimum(pos + w // 2 - 1, S - 1)
        win_sum = cs[:, hi + 1, gi] - cs[:, lo, gi]
        cnt = (hi - lo + 1).astype(jnp.float32)[None, :, None]
        outs.append(win_sum / cnt - g[:, :, gi])
    pooled = jnp.stack(outs, axis=2).astype(ua.dtype)
    mixed = jnp.einsum('bsgc,gce->bsge', pooled, pool_w)
    return mixed.reshape(B, S, D_POOL) * pool_scale


def _diff_attention(q, k, v, lam, lam_init, subln_w):
    B, S = q.shape[0], q.shape[1]
    q = _rope(q, S)
    k = _rope(k, S)
    n_blk = S // Q_BLOCK
    qb = q.reshape(B, n_blk, Q_BLOCK, N_HEADS, 2, HEAD_DIM).transpose(1, 0, 2, 3, 4, 5)
    scale = HEAD_DIM ** -0.5

    def block(qi):
        s = jnp.einsum('bqhjd,bkhjd->bhjqk', qi, k).astype(jnp.float32) * scale
        p = jax.nn.softmax(s, axis=-1)
        a = (p[:, :, 0] - lam * p[:, :, 1]).astype(v.dtype)
        return jnp.einsum('bhqk,bkhe->bqhe', a, v)

    o = lax.map(block, qb)
    o = o.transpose(1, 0, 2, 3, 4).reshape(B, S, N_HEADS, 2 * HEAD_DIM)
    of = o.astype(jnp.float32)
    of = of * lax.rsqrt(jnp.mean(jnp.square(of), axis=-1, keepdims=True) + LN_EPS)
    of = of * subln_w.astype(jnp.float32) * (1.0 - lam_init)
    return of.astype(v.dtype).reshape(B, S, N_HEADS * 2 * HEAD_DIM)


def _moe(u, router_w, router_b, w_up, b_up, w_down, b_down):
    B, S, D = u.shape
    t = u.reshape(-1, D)
    n_tok = t.shape[0]
    logits = (t @ router_w + router_b).astype(jnp.float32)
    top_v, top_i = lax.top_k(logits, TOP_K)
    gates = jax.nn.softmax(top_v, axis=-1).astype(t.dtype)
    n_asg = n_tok * TOP_K
    e_flat = top_i.reshape(-1)
    order = jnp.argsort(e_flat)
    e_sorted = e_flat[order]
    tok_sorted = (order // TOP_K).astype(jnp.int32)
    g_sorted = gates.reshape(-1)[order]
    counts = jnp.bincount(e_flat, length=N_EXPERTS)
    starts = jnp.cumsum(counts) - counts
    padded = (counts + EXPERT_BLOCK - 1) // EXPERT_BLOCK * EXPERT_BLOCK
    pstarts = jnp.cumsum(padded) - padded
    pends = pstarts + padded
    dest = pstarts[e_sorted] + (jnp.arange(n_asg) - starts[e_sorted])
    n_rows = ((n_asg + EXPERT_BLOCK - 1) // EXPERT_BLOCK + N_EXPERTS) * EXPERT_BLOCK
    n_blocks = n_rows // EXPERT_BLOCK
    tok_buf = jnp.zeros((n_rows,), jnp.int32).at[dest].set(tok_sorted)
    g_buf = jnp.zeros((n_rows,), t.dtype).at[dest].set(g_sorted)
    blk_e = jnp.minimum(jnp.searchsorted(pends, jnp.arange(n_blocks) * EXPERT_BLOCK, side='right'),
                        N_EXPERTS - 1).astype(jnp.int32)
    xb = t[tok_buf].reshape(n_blocks, EXPERT_BLOCK, D)

    def expert_block(args):
        xt, e = args
        h = xt @ w_up[e] + b_up[e]
        gate = jnp.minimum(h[:, :D_FF], SWIGLU_LIMIT)
        up = jnp.clip(h[:, D_FF:], -SWIGLU_LIMIT, SWIGLU_LIMIT)
        act = (up + 1) * (gate * jax.nn.sigmoid(SWIGLU_ALPHA * gate))
        return act @ w_down[e] + b_down[e]

    yb = lax.map(expert_block, (xb, blk_e)).reshape(n_rows, D)
    out = jnp.zeros_like(t).at[tok_buf].add(yb * g_buf[:, None])
    return out.reshape(B, S, D)


def _encode(x, c, w_ada, b_ada, w_in, pool_w, pool_scale, w_branch_a, w_branch_b,
            lambda_q1, lambda_k1, lambda_q2, lambda_k2, subln_w, w_out, ln1_g, ln1_b,
            router_w, router_b, exp_w_up, exp_b_up, exp_w_down, exp_b_down, ln2_g, ln2_b):
    B, S, _ = x.shape
    for i in range(DEPTH):
        lam_init = 0.8 - 0.6 * math.exp(-0.3 * i)
        mod = jax.nn.silu(c) @ w_ada[i] + b_ada[i]
        sh1, sc1, g1, sh2, sc2, g2 = jnp.split(mod, 6, axis=-1)
        u = _modulate(_layernorm_plain(x), sh1, sc1)
        proj = u @ w_in[i]
        cuts = [D_POOL, D_POOL + QK_W, D_POOL + 2 * QK_W, D_POOL + 2 * QK_W + V_W,
                D_POOL + 2 * QK_W + V_W + D_MODEL]
        ua, q, k, v, ga, gb = jnp.split(proj, cuts, axis=-1)
        ya = _multiscale_pool(ua, pool_w[i], pool_scale[i]) @ w_branch_a[i]
        lam = (jnp.exp(jnp.sum(lambda_q1[i].astype(jnp.float32) * lambda_k1[i].astype(jnp.float32)))
               - jnp.exp(jnp.sum(lambda_q2[i].astype(jnp.float32) * lambda_k2[i].astype(jnp.float32)))
               + lam_init)
        yattn = _diff_attention(q.reshape(B, S, N_HEADS, 2, HEAD_DIM),
                                k.reshape(B, S, N_HEADS, 2, HEAD_DIM),
                                v.reshape(B, S, N_HEADS, 2 * HEAD_DIM),
                                lam, lam_init, subln_w[i])
        yb = yattn @ w_branch_b[i]
        merged = jax.nn.sigmoid(ga) * ya + jax.nn.sigmoid(gb) * yb
        h = merged @ w_out[i]
        x = _layernorm(DEEPNORM_ALPHA * x + g1[:, None, :] * h, ln1_g[i], ln1_b[i])
        u2 = _modulate(_layernorm_plain(x), sh2, sc2)
        h2 = _moe(u2, router_w[i], router_b[i], exp_w_up[i], exp_b_up[i], exp_w_down[i], exp_b_down[i])
        x = _layernorm(DEEPNORM_ALPHA * x + g2[:, None, :] * h2, ln2_g[i], ln2_b[i])
    return x


def setup_inputs(seed: int = 0) -> dict:
    key = jax.random.key(seed)
    ks = jax.random.split(key, 32)
    f32 = jnp.float32
    D, L, E, F = D_MODEL, DEPTH, N_EXPERTS, D_FF

    def nrm(k, shape, scale):
        return jax.random.normal(k, shape, f32) * scale

    return {
        "x_prompt": nrm(ks[0], (BATCH, SEQ, D), 1.0),
        "x_sample": nrm(ks[1], (DEC_BATCH, DEC_SEQ, D), 1.0),
        "c_prompt": nrm(ks[2], (BATCH, D), 1.0),
        "c_sample": nrm(ks[3], (DEC_BATCH, D), 1.0),
        "w_ada": nrm(ks[4], (L, D, 6 * D), 0.5 * D ** -0.5),
        "b_ada": nrm(ks[5], (L, 6 * D), 0.02),
        "w_in": nrm(ks[6], (L, D, IN_COLS), D ** -0.5),
        "pool_w": nrm(ks[7], (L, N_POOL_GROUPS, POOL_GC, POOL_GC), POOL_GC ** -0.5),
        "pool_scale": 1.0 + nrm(ks[8], (L, D_POOL), 0.1),
        "w_branch_a": nrm(ks[9], (L, D_POOL, D), DEEPNORM_BETA * D_POOL ** -0.5),
        "w_branch_b": nrm(ks[10], (L, V_W, D), DEEPNORM_BETA * V_W ** -0.5),
        "lambda_q1": nrm(ks[11], (L, HEAD_DIM), 0.1),
        "lambda_k1": nrm(ks[12], (L, HEAD_DIM), 0.1),
        "lambda_q2": nrm(ks[13], (L, HEAD_DIM), 0.1),
        "lambda_k2": nrm(ks[14], (L, HEAD_DIM), 0.1),
        "subln_w": 1.0 + nrm(ks[15], (L, 2 * HEAD_DIM), 0.05),
        "w_out": nrm(ks[16], (L, D, D), DEEPNORM_BETA * D ** -0.5),
        "ln1_g": 1.0 + nrm(ks[17], (L, D), 0.05),
        "ln1_b": nrm(ks[18], (L, D), 0.02),
        "router_w": nrm(ks[19], (L, D, E), D ** -0.5),
        "router_b": nrm(ks[20], (L, E), 0.01),
        "exp_w_up": nrm(ks[21], (L, E, D, 2 * F), D ** -0.5),
        "exp_b_up": nrm(ks[22], (L, E, 2 * F), 0.02),
        "exp_w_down": nrm(ks[23], (L, E, F, D), DEEPNORM_BETA * F ** -0.5),
        "exp_b_down": nrm(ks[24], (L, E, D), 0.02),
        "ln2_g": 1.0 + nrm(ks[25], (L, D), 0.05),
        "ln2_b": nrm(ks[26], (L, D), 0.02),
    }


def reference(x_prompt, x_sample, c_prompt, c_sample, w_ada, b_ada, w_in, pool_w, pool_scale,
              w_branch_a, w_branch_b, lambda_q1, lambda_k1, lambda_q2, lambda_k2, subln_w, w_out,
              ln1_g, ln1_b, router_w, router_b, exp_w_up, exp_b_up, exp_w_down, exp_b_down,
              ln2_g, ln2_b):
    y_prompt = _encode(x_prompt, c_prompt, w_ada, b_ada, w_in, pool_w, pool_scale, w_branch_a,
                       w_branch_b, lambda_q1, lambda_k1, lambda_q2, lambda_k2, subln_w, w_out,
                       ln1_g, ln1_b, router_w, router_b, exp_w_up, exp_b_up, exp_w_down,
                       exp_b_down, ln2_g, ln2_b)
    y_sample = _encode(x_sample, c_sample, w_ada, b_ada, w_in, pool_w, pool_scale, w_branch_a,
                       w_branch_b, lambda_q1, lambda_k1, lambda_q2, lambda_k2, subln_w, w_out,
                       ln1_g, ln1_b, router_w, router_b, exp_w_up, exp_b_up, exp_w_down,
                       exp_b_down, ln2_g, ln2_b)
    return (y_prompt, y_sample)
```

```python
import functools

import jax
import jax.numpy as jnp
from jax import lax
from jax.experimental import pallas as pl
from jax.experimental.pallas import tpu as pltpu

F32 = jnp.float32
BF16 = jnp.bfloat16

D_MODEL = 1024
N_HEADS = 8
HEAD_DIM = 64
HEAD_W = 2 * HEAD_DIM
ROPE_THETA = 10000.0
POOL_WINDOWS = (2, 4, 8, 16)
POOL_GC = 256
POOL_HALO = 8
N_EXPERTS = 32
TOP_K = 4
D_FF = 1024
SWIGLU_LIMIT = 7.0
SWIGLU_ALPHA = 1.702
LN_EPS = 1e-5
DEPTH = 1
DEEPNORM_ALPHA = (2.0 * DEPTH) ** 0.25
LAMBDA_INIT = 0.8 - 0.6 * 1.0

VMEM_LIMIT_V7X = 56 * 1024 * 1024

TOKEN_TILE = 256
ATTN_Q_TILE = 512
ATTN_K_CHUNK = 512
EXPERT_ROWS = 256
DISPATCH_TILE = 256
COMBINE_TILE = 256


def _params(n_axes, **kw):
    return pltpu.CompilerParams(dimension_semantics=("arbitrary",) * n_axes,
                                vmem_limit_bytes=VMEM_LIMIT_V7X, **kw)


def _ln_plain(x):
    mu = jnp.mean(x, axis=-1, keepdims=True)
    xc = x - mu
    var = jnp.mean(xc * xc, axis=-1, keepdims=True)
    return xc * lax.rsqrt(var + LN_EPS)


def _dot(a, b):
    return jnp.dot(a, b, preferred_element_type=F32)


def _dot_nt(a, b, **kw):
    return lax.dot_general(a, b, (((1,), (1,)), ((), ())), preferred_element_type=F32, **kw)


def _mod_kernel(c_ref, w_ref, b_ref, o_ref):
    c = c_ref[...]
    s = c * jax.nn.sigmoid(c)
    o_ref[...] = jnp.dot(s, w_ref[...], preferred_element_type=F32,
                         precision=lax.Precision.HIGHEST) + b_ref[...]


def _mod_call(c, w_ada, b_ada):
    rows, d = c.shape
    n_out = w_ada.shape[1]
    return pl.pallas_call(
        _mod_kernel,
        grid=(n_out // d,),
        in_specs=[pl.BlockSpec((rows, d), lambda j: (0, 0)),
                  pl.BlockSpec((d, d), lambda j: (0, j)),
                  pl.BlockSpec((1, d), lambda j: (0, j))],
        out_specs=pl.BlockSpec((rows, d), lambda j: (0, j)),
        out_shape=jax.ShapeDtypeStruct((rows, n_out), F32),
        compiler_params=_params(1),
        name="mod",
    )(c, w_ada, b_ada.reshape(1, n_out))


def _inproj_kernel(x_ref, mod_ref, cos_ref, sina_ref, sinb_ref, wcat_ref, wvt_ref,
                   ua_ref, q_ref, k_ref, vt_ref, sga_ref, sgb_ref):
    d = D_MODEL
    x = x_ref[0]
    sh1 = mod_ref[0, 0:1, :]
    sc1 = mod_ref[0, 1:2, :]
    u = (_ln_plain(x) * (1.0 + sc1) + sh1).astype(BF16)

    def proj(j):
        return _dot(u, wcat_ref[:, j * d:(j + 1) * d])

    reps = d // HEAD_W
    cos = jnp.concatenate([cos_ref[...]] * reps, axis=1)
    sina = jnp.concatenate([sina_ref[...]] * reps, axis=1)
    sinb = jnp.concatenate([sinb_ref[...]] * reps, axis=1)

    def rope(t):
        half = HEAD_DIM // 2
        return t * cos + pltpu.roll(t, d - half, 1) * sina + pltpu.roll(t, half, 1) * sinb

    ua_ref[0] = proj(0)
    q_ref[0] = (rope(proj(1)) * (HEAD_DIM ** -0.5)).astype(BF16)
    k_ref[0] = rope(proj(2)).astype(BF16)
    vt_ref[0] = _dot_nt(wvt_ref[...], u).astype(BF16)
    sga_ref[0] = jax.nn.sigmoid(proj(3))
    sgb_ref[0] = jax.nn.sigmoid(proj(4))


def _inproj_call(x, mod, cos, sina, sinb, wcat, wvt):
    b, s, d = x.shape
    tm = min(TOKEN_TILE, s)
    row = lambda bi, i: (bi, i, 0)
    const2 = lambda bi, i: (0, 0)
    tab = pl.BlockSpec((tm, HEAD_W), lambda bi, i: (i, 0))
    return pl.pallas_call(
        _inproj_kernel,
        grid=(b, s // tm),
        in_specs=[pl.BlockSpec((1, tm, d), row),
                  pl.BlockSpec((1, 6, d), lambda bi, i: (bi, 0, 0)),
                  tab, tab, tab,
                  pl.BlockSpec(wcat.shape, const2),
                  pl.BlockSpec(wvt.shape, const2)],
        out_specs=[pl.BlockSpec((1, tm, d), row),
                   pl.BlockSpec((1, tm, d), row),
                   pl.BlockSpec((1, tm, d), row),
                   pl.BlockSpec((1, d, tm), lambda bi, i: (bi, 0, i)),
                   pl.BlockSpec((1, tm, d), row),
                   pl.BlockSpec((1, tm, d), row)],
        out_shape=[jax.ShapeDtypeStruct((b, s, d), F32),
                   jax.ShapeDtypeStruct((b, s, d), BF16),
                   jax.ShapeDtypeStruct((b, s, d), BF16),
                   jax.ShapeDtypeStruct((b, d, s), BF16),
                   jax.ShapeDtypeStruct((b, s, d), F32),
                   jax.ShapeDtypeStruct((b, s, d), F32)],
        compiler_params=_params(2),
        name="inproj",
    )(x, mod, cos, sina, sinb, wcat, wvt)


def _attn_kernel(q_ref, k_ref, vt_ref, lq1_ref, lk1_ref, lq2_ref, lk2_ref, subw_ref,
                 o_ref, o1_ref, o2_ref, *, k_chunk):
    tq = q_ref.shape[1]
    s_len = k_ref.shape[1]
    q = q_ref[0]
    lane = lax.broadcasted_iota(jnp.int32, q.shape, 1)
    zero = jnp.zeros_like(q)
    q1 = jnp.where(lane < HEAD_DIM, q, zero)
    q2 = jnp.where(lane >= HEAD_DIM, q, zero)
    o1_ref[...] = jnp.zeros_like(o1_ref)
    o2_ref[...] = jnp.zeros_like(o2_ref)

    def online(s, m, l, o_ref_, vtc):
        m_new = jnp.maximum(m, jnp.max(s, axis=0, keepdims=True))
        alpha = jnp.exp(m - m_new)
        p = jnp.exp(s - m_new)
        l_new = alpha * l + jnp.sum(p, axis=0, keepdims=True)
        o_ref_[...] = alpha * o_ref_[...] + _dot(vtc, p.astype(BF16))
        return m_new, l_new

    def body(c, carry):
        m1, l1, m2, l2 = carry
        start = pl.multiple_of(c * k_chunk, k_chunk)
        kc = k_ref[0, pl.ds(start, k_chunk), :]
        vtc = vt_ref[0, :, pl.ds(start, k_chunk)]
        m1, l1 = online(_dot_nt(kc, q1), m1, l1, o1_ref, vtc)
        m2, l2 = online(_dot_nt(kc, q2), m2, l2, o2_ref, vtc)
        return m1, l1, m2, l2

    neg = jnp.full((1, tq), -jnp.inf, F32)
    zer = jnp.zeros((1, tq), F32)
    _, l1, _, l2 = lax.fori_loop(0, s_len // k_chunk, body, (neg, zer, neg, zer))

    lam = (jnp.exp(jnp.sum(lq1_ref[...] * lk1_ref[...], axis=1, keepdims=True))
           - jnp.exp(jnp.sum(lq2_ref[...] * lk2_ref[...], axis=1, keepdims=True))
           + LAMBDA_INIT)
    o = o1_ref[...] / l1 - lam * (o2_ref[...] / l2)
    o = o * lax.rsqrt(jnp.mean(o * o, axis=0, keepdims=True) + LN_EPS)
    o = o * subw_ref[...] * (1.0 - LAMBDA_INIT)
    o_ref[0] = o.T.astype(o_ref.dtype)


def _attn_call(q, k, vt, lq1, lk1, lq2, lk2, subw):
    b, s, d = q.shape
    tq = min(ATTN_Q_TILE, s)
    kc = min(ATTN_K_CHUNK, s)
    lam_spec = pl.BlockSpec((1, HEAD_DIM), lambda bi, h, i: (0, 0))
    return pl.pallas_call(
        functools.partial(_attn_kernel, k_chunk=kc),
        grid=(b, N_HEADS, s // tq),
        in_specs=[pl.BlockSpec((1, tq, HEAD_W), lambda bi, h, i: (bi, i, h)),
                  pl.BlockSpec((1, s, HEAD_W), lambda bi, h, i: (bi, 0, h)),
                  pl.BlockSpec((1, HEAD_W, s), lambda bi, h, i: (bi, h, 0)),
                  lam_spec, lam_spec, lam_spec, lam_spec,
                  pl.BlockSpec((HEAD_W, 1), lambda bi, h, i: (0, 0))],
        out_specs=pl.BlockSpec((1, tq, HEAD_W), lambda bi, h, i: (bi, i, h)),
        out_shape=jax.ShapeDtypeStruct((b, s, d), BF16),
        scratch_shapes=[pltpu.VMEM((HEAD_W, tq), F32), pltpu.VMEM((HEAD_W, tq), F32)],
        compiler_params=_params(3),
        name="attn",
    )(q, k, vt, lq1, lk1, lq2, lk2, subw)


def _mixer_kernel(x_ref, ua_ref, uprev_ref, unext_ref, sga_ref, sgb_ref, yattn_ref, mod_ref,
                  poolw_ref, pscale_ref, wa_ref, wb_ref, wout_ref, g_ref, b_ref, rwt_ref, rb_ref,
                  x1_ref, eidx_ref, rank_ref, gate_ref, cnt_ref,
                  ext_ref, run_ref, *, seq_len):
    tm = x_ref.shape[1]
    halo = POOL_HALO
    bi = pl.program_id(0)
    i = pl.program_id(1)
    n_i = pl.num_programs(1)

    @pl.when((bi == 0) & (i == 0))
    def _():
        run_ref[...] = jnp.zeros_like(run_ref)

    ua = ua_ref[0]
    ext_ref[0:halo, :] = jnp.where(i > 0, uprev_ref[0], 0.0)
    ext_ref[halo:halo + tm, :] = ua
    ext_ref[halo + tm:, :] = jnp.where(i < n_i - 1, unext_ref[0], 0.0)
    pos = i * tm + lax.broadcasted_iota(jnp.int32, (tm, 1), 0)
    mixed = []
    for g, w in enumerate(POOL_WINDOWS):
        cols = slice(g * POOL_GC, (g + 1) * POOL_GC)
        win = ext_ref[halo - w // 2:halo - w // 2 + tm, cols]
        for j in range(1, w):
            off = halo - w // 2 + j
            win = win + ext_ref[off:off + tm, cols]
        lo = jnp.maximum(pos - w // 2, 0)
        hi = jnp.minimum(pos + w // 2 - 1, seq_len - 1)
        cnt = (hi - lo + 1).astype(F32)
        pooled = win / cnt - ua[:, cols]
        mixed.append(_dot(pooled.astype(BF16), poolw_ref[g]))
    mixed = jnp.concatenate(mixed, axis=1) * pscale_ref[...]
    ya = _dot(mixed.astype(BF16), wa_ref[...])

    yb = _dot(yattn_ref[0], wb_ref[...])
    merged = sga_ref[0] * ya + sgb_ref[0] * yb
    h = _dot(merged.astype(BF16), wout_ref[...])
    g1 = mod_ref[0, 2:3, :]
    x1 = _ln_plain(DEEPNORM_ALPHA * x_ref[0] + g1 * h) * g_ref[...] + b_ref[...]
    x1_ref[0] = x1

    sh2 = mod_ref[0, 3:4, :]
    sc2 = mod_ref[0, 4:5, :]
    u2 = _ln_plain(x1) * (1.0 + sc2) + sh2
    logits = _dot_nt(rwt_ref[...], u2, precision=lax.Precision.HIGHEST) + rb_ref[...]
    eio = lax.broadcasted_iota(jnp.int32, logits.shape, 0)
    vals, idxs, sels = [], [], []
    l = logits
    for _ in range(TOP_K):
        mv = jnp.max(l, axis=0, keepdims=True)
        idx = jnp.min(jnp.where(l == mv, eio, N_EXPERTS), axis=0, keepdims=True)
        sel = eio == idx
        vals.append(mv)
        idxs.append(idx)
        sels.append(sel)
        l = jnp.where(sel, -jnp.inf, l)
    ex = [jnp.exp(v - vals[0]) for v in vals]
    den = ex[0] + ex[1] + ex[2] + ex[3]
    gate_ref[...] = jnp.concatenate([e / den for e in ex], axis=0)
    eidx_ref[...] = jnp.concatenate(idxs, axis=0)

    member = jnp.zeros(logits.shape, F32)
    for sel in sels:
        member = member + sel.astype(F32)
    tri = (lax.broadcasted_iota(jnp.int32, (tm, tm), 0)
           < lax.broadcasted_iota(jnp.int32, (tm, tm), 1)).astype(BF16)
    base = run_ref[...] + _dot(member.astype(BF16), tri)
    ranks = [jnp.sum(jnp.where(sel, base, 0.0), axis=0, keepdims=True) for sel in sels]
    rank_ref[...] = jnp.concatenate(ranks, axis=0).astype(jnp.int32)
    run_new = run_ref[...] + jnp.sum(member, axis=1, keepdims=True)
    run_ref[...] = run_new
    cnt_ref[...] = jnp.broadcast_to(run_new, cnt_ref.shape)


def _mixer_call(x, ua, sga, sgb, yattn, mod, poolw, pscale, wa, wb, wout, g, b_, rwt, rb):
    b, s, d = x.shape
    tm = min(TOKEN_TILE, s)
    n_i = s // tm
    hb = tm // POOL_HALO
    n_hb = s // POOL_HALO
    row = lambda bi, i: (bi, i, 0)
    c2 = lambda bi, i: (0, 0)
    c3 = lambda bi, i: (0, 0, 0)
    tokcol = lambda bi, i: (0, bi * n_i + i)
    n_tok = b * s
    return pl.pallas_call(
        functools.partial(_mixer_kernel, seq_len=s),
        grid=(b, n_i),
        in_specs=[pl.BlockSpec((1, tm, d), row),
                  pl.BlockSpec((1, tm, d), row),
                  pl.BlockSpec((1, POOL_HALO, d), lambda bi, i: (bi, jnp.maximum(i * hb - 1, 0), 0)),
                  pl.BlockSpec((1, POOL_HALO, d), lambda bi, i: (bi, jnp.minimum((i + 1) * hb, n_hb - 1), 0)),
                  pl.BlockSpec((1, tm, d), row),
                  pl.BlockSpec((1, tm, d), row),
                  pl.BlockSpec((1, tm, d), row),
                  pl.BlockSpec((1, 6, d), lambda bi, i: (bi, 0, 0)),
                  pl.BlockSpec(poolw.shape, c3),
                  pl.BlockSpec((1, d), c2),
                  pl.BlockSpec((d, d), c2),
                  pl.BlockSpec((d, d), c2),
                  pl.BlockSpec((d, d), c2),
                  pl.BlockSpec((1, d), c2),
                  pl.BlockSpec((1, d), c2),
                  pl.BlockSpec((N_EXPERTS, d), c2),
                  pl.BlockSpec((N_EXPERTS, 1), c2)],
        out_specs=[pl.BlockSpec((1, tm, d), row),
                   pl.BlockSpec((TOP_K, tm), tokcol),
                   pl.BlockSpec((TOP_K, tm), tokcol),
                   pl.BlockSpec((TOP_K, tm), tokcol),
                   pl.BlockSpec((N_EXPERTS, 128), c2)],
        out_shape=[jax.ShapeDtypeStruct((b, s, d), F32),
                   jax.ShapeDtypeStruct((TOP_K, n_tok), jnp.int32),
                   jax.ShapeDtypeStruct((TOP_K, n_tok), jnp.int32),
                   jax.ShapeDtypeStruct((TOP_K, n_tok), F32),
                   jax.ShapeDtypeStruct((N_EXPERTS, 128), F32)],
        scratch_shapes=[pltpu.VMEM((tm + 2 * POOL_HALO, d), F32),
                        pltpu.VMEM((N_EXPERTS, 1), F32)],
        compiler_params=_params(2),
        name="mixer",
    )(x, ua, ua, ua, sga, sgb, yattn, mod, poolw, pscale, wa, wb, wout, g, b_, rwt, rb)


def _row_copy(src_ref, src_row, dst_ref, dst_row, sem):
    return pltpu.make_async_copy(src_ref.at[pl.ds(src_row, 1)], dst_ref.at[pl.ds(dst_row, 1)], sem)


def _dispatch_kernel(dest_ref, x1_ref, mod_ref, xb_in_ref, xb_ref, u_ref, sem):
    del xb_in_ref
    td = x1_ref.shape[0]
    sh2 = mod_ref[0, 3:4, :]
    sc2 = mod_ref[0, 4:5, :]
    u_ref[...] = _ln_plain(x1_ref[...]) * (1.0 + sc2) + sh2

    def issue(t, c):
        for k in range(TOP_K):
            _row_copy(u_ref, t, xb_ref, dest_ref[k, t], sem).start()
        return c

    lax.fori_loop(0, td, issue, 0)

    def drain(t, c):
        for k in range(TOP_K):
            _row_copy(u_ref, 0, xb_ref, 0, sem).wait()
        return c

    lax.fori_loop(0, td, drain, 0)


def _dispatch_call(dest, x1_flat, mod, xb_zero, seq_len):
    n_tok, d = x1_flat.shape
    td = min(DISPATCH_TILE, seq_len)
    return pl.pallas_call(
        _dispatch_kernel,
        grid=(n_tok // td,),
        in_specs=[pl.BlockSpec((TOP_K, td), lambda i: (0, i), memory_space=pltpu.SMEM),
                  pl.BlockSpec((td, d), lambda i: (i, 0)),
                  pl.BlockSpec((1, 6, d), lambda i: ((i * td) // seq_len, 0, 0)),
                  pl.BlockSpec(memory_space=pl.ANY)],
        out_specs=pl.BlockSpec(memory_space=pl.ANY),
        out_shape=jax.ShapeDtypeStruct(xb_zero.shape, F32),
        scratch_shapes=[pltpu.VMEM((td, d), F32), pltpu.SemaphoreType.DMA],
        input_output_aliases={3: 0},
        compiler_params=_params(1, has_side_effects=True),
        name="dispatch",
    )(dest, x1_flat, mod, xb_zero)


def _expert_kernel(blk_e_ref, n_used_ref, xb_ref, wup_ref, bup_ref, wdn_ref, bdn_ref, yb_ref):
    del blk_e_ref
    used = pl.program_id(0) < n_used_ref[0]

    @pl.when(jnp.logical_not(used))
    def _():
        yb_ref[...] = jnp.zeros_like(yb_ref)

    @pl.when(used)
    def _():
        x = xb_ref[...].astype(BF16)
        h = _dot(x, wup_ref[0]) + bup_ref[0]
        gate = jnp.minimum(h[:, :D_FF], SWIGLU_LIMIT)
        up = jnp.clip(h[:, D_FF:], -SWIGLU_LIMIT, SWIGLU_LIMIT)
        act = (up + 1.0) * (gate * jax.nn.sigmoid(SWIGLU_ALPHA * gate))
        yb_ref[...] = _dot(act.astype(BF16), wdn_ref[0]) + bdn_ref[0]


def _expert_call(blk_e, n_used, xb, wup, bup, wdn, bdn):
    n_rows, d = xb.shape
    rows = EXPERT_ROWS
    last = lambda r, be, nu: jnp.minimum(r, nu[0] - 1)
    rowblk = lambda r, be, nu: (last(r, be, nu), 0)
    exp3 = lambda r, be, nu: (be[last(r, be, nu)], 0, 0)
    return pl.pallas_call(
        _expert_kernel,
        grid_spec=pltpu.PrefetchScalarGridSpec(
            num_scalar_prefetch=2,
            grid=(n_rows // rows,),
            in_specs=[pl.BlockSpec((rows, d), rowblk),
                      pl.BlockSpec((1, d, 2 * D_FF), exp3),
                      pl.BlockSpec((1, 1, 2 * D_FF), exp3),
                      pl.BlockSpec((1, D_FF, d), exp3),
                      pl.BlockSpec((1, 1, d), exp3)],
            out_specs=pl.BlockSpec((rows, d), lambda r, be, nu: (r, 0))),
        out_shape=jax.ShapeDtypeStruct((n_rows, d), F32),
        compiler_params=_params(1),
        name="experts",
    )(blk_e, n_used, xb, wup, bup, wdn, bdn)


def _combine_kernel(dest_ref, x1_ref, gates_ref, mod_ref, g_ref, b_ref, yb_ref, o_ref, buf_ref, sem):
    tc = x1_ref.shape[0]

    def issue(t, c):
        for k in range(TOP_K):
            _row_copy(yb_ref, dest_ref[k, t], buf_ref.at[k], t, sem).start()
        return c

    lax.fori_loop(0, tc, issue, 0)

    def drain(t, c):
        for k in range(TOP_K):
            _row_copy(yb_ref, 0, buf_ref.at[k], 0, sem).wait()
        return c

    lax.fori_loop(0, tc, drain, 0)

    gates = gates_ref[...]
    h2 = gates[:, 0:1] * buf_ref[0]
    for k in range(1, TOP_K):
        h2 = h2 + gates[:, k:k + 1] * buf_ref[k]
    g2 = mod_ref[0, 5:6, :]
    o_ref[...] = _ln_plain(DEEPNORM_ALPHA * x1_ref[...] + g2 * h2) * g_ref[...] + b_ref[...]


def _combine_call(dest, x1_flat, gates_t, mod, g, b_, yb, seq_len):
    n_tok, d = x1_flat.shape
    tc = min(COMBINE_TILE, seq_len)
    c2 = lambda i: (0, 0)
    return pl.pallas_call(
        _combine_kernel,
        grid=(n_tok // tc,),
        in_specs=[pl.BlockSpec((TOP_K, tc), lambda i: (0, i), memory_space=pltpu.SMEM),
                  pl.BlockSpec((tc, d), lambda i: (i, 0)),
                  pl.BlockSpec((tc, TOP_K), lambda i: (i, 0)),
                  pl.BlockSpec((1, 6, d), lambda i: ((i * tc) // seq_len, 0, 0)),
                  pl.BlockSpec((1, d), c2),
                  pl.BlockSpec((1, d), c2),
                  pl.BlockSpec(memory_space=pl.ANY)],
        out_specs=pl.BlockSpec((tc, d), lambda i: (i, 0)),
        out_shape=jax.ShapeDtypeStruct((n_tok, d), F32),
        scratch_shapes=[pltpu.VMEM((TOP_K, tc, d), F32), pltpu.SemaphoreType.DMA],
        compiler_params=_params(1),
        name="combine",
    )(dest, x1_flat, gates_t, mod, g, b_, yb)


def _rope_tables(seq_len):
    inv_freq = ROPE_THETA ** (-jnp.arange(0, HEAD_DIM, 2, dtype=F32) / HEAD_DIM)
    ang = jnp.arange(seq_len, dtype=F32)[:, None] * inv_freq[None, :]
    cos, sin = jnp.cos(ang), jnp.sin(ang)
    zero = jnp.zeros_like(sin)
    reps = HEAD_W // HEAD_DIM
    cos_t = jnp.tile(jnp.concatenate([cos, cos], axis=1), (1, reps))
    sina_t = jnp.tile(jnp.concatenate([-sin, zero], axis=1), (1, reps))
    sinb_t = jnp.tile(jnp.concatenate([zero, sin], axis=1), (1, reps))
    return cos_t, sina_t, sinb_t


def _encode(x, mod, w):
    b, s, d = x.shape
    n_tok = b * s
    cos, sina, sinb = _rope_tables(s)
    ua, q, k, vt, sga, sgb = _inproj_call(x, mod, cos, sina, sinb, w["wcat"], w["wvt"])
    yattn = _attn_call(q, k, vt, w["lq1"], w["lk1"], w["lq2"], w["lk2"], w["subw"])
    x1, eidx, rank, gates, cnt = _mixer_call(
        x, ua, sga, sgb, yattn, mod, w["poolw"], w["pscale"], w["wa"], w["wb"], w["wout"],
        w["ln1_g"], w["ln1_b"], w["rwt"], w["rb"])

    rows = EXPERT_ROWS
    counts = cnt[:, 0].astype(jnp.int32)
    padded = (counts + rows - 1) // rows * rows
    pends = jnp.cumsum(padded)
    pstarts = pends - padded
    dest = jnp.take(pstarts, eidx, axis=0) + rank
    n_blocks = (n_tok * TOP_K) // rows + N_EXPERTS
    blk_e = jnp.minimum(jnp.searchsorted(pends, jnp.arange(n_blocks, dtype=jnp.int32) * rows, side="right"),
                        N_EXPERTS - 1).astype(jnp.int32)
    n_used = (pends[-1:] // rows).astype(jnp.int32)

    x1_flat = x1.reshape(n_tok, d)
    xb = _dispatch_call(dest, x1_flat, mod, jnp.zeros((n_blocks * rows, d), F32), s)
    yb = _expert_call(blk_e, n_used, xb, w["wup"], w["bup"], w["wdn"], w["bdn"])
    out = _combine_call(dest, x1_flat, gates.T, mod, w["ln2_g"], w["ln2_b"], yb, s)
    return out.reshape(b, s, d)


def kernel(x_prompt, x_sample, c_prompt, c_sample, w_ada, b_ada, w_in, pool_w, pool_scale, w_branch_a, w_branch_b, lambda_q1, lambda_k1, lambda_q2, lambda_k2, subln_w, w_out, ln1_g, ln1_b, router_w, router_b, exp_w_up, exp_b_up, exp_w_down, exp_b_down, ln2_g, ln2_b):
    d = D_MODEL
    w_in0 = w_in[0]
    seg = lambda j: w_in0[:, j * d:(j + 1) * d]
    w = {
        "wcat": jnp.concatenate([seg(0), seg(1), seg(2), seg(4), seg(5)], axis=1).astype(BF16),
        "wvt": seg(3).T.astype(BF16),
        "lq1": lambda_q1, "lk1": lambda_k1, "lq2": lambda_q2, "lk2": lambda_k2,
        "subw": subln_w[0].reshape(HEAD_W, 1),
        "poolw": pool_w[0].astype(BF16),
        "pscale": pool_scale,
        "wa": w_branch_a[0].astype(BF16),
        "wb": w_branch_b[0].astype(BF16),
        "wout": w_out[0].astype(BF16),
        "ln1_g": ln1_g, "ln1_b": ln1_b,
        "rwt": router_w[0].T,
        "rb": router_b[0].reshape(N_EXPERTS, 1),
        "wup": exp_w_up[0].astype(BF16),
        "bup": exp_b_up[0].reshape(N_EXPERTS, 1, 2 * D_FF),
        "wdn": exp_w_down[0].astype(BF16),
        "bdn": exp_b_down[0].reshape(N_EXPERTS, 1, d),
        "ln2_g": ln2_g, "ln2_b": ln2_b,
    }
    nb_p, nb_s = c_prompt.shape[0], c_sample.shape[0]
    pad = (-(nb_p + nb_s)) % 8
    c_all = jnp.concatenate([c_prompt, c_sample, jnp.zeros((pad, d), F32)], axis=0)
    mod = _mod_call(c_all, w_ada[0], b_ada[0]).reshape(-1, 6, d)
    y_prompt = _encode(x_prompt, mod[:nb_p], w)
    y_sample = _encode(x_sample, mod[nb_p:nb_p + nb_s], w)
    return (y_prompt, y_sample)
```

```python
import functools

import jax
import jax.numpy as jnp
from jax import lax
from jax.experimental import pallas as pl
from jax.experimental.pallas import tpu as pltpu

F32 = jnp.float32
BF16 = jnp.bfloat16

D_MODEL = 1024
N_HEADS = 8
HEAD_DIM = 64
HEAD_W = 2 * HEAD_DIM
ROPE_THETA = 10000.0
POOL_WINDOWS = (2, 4, 8, 16)
POOL_GC = 256
POOL_HALO = 8
N_EXPERTS = 32
TOP_K = 4
D_FF = 1024
SWIGLU_LIMIT = 7.0
SWIGLU_ALPHA = 1.702
LN_EPS = 1e-5
DEPTH = 1
DEEPNORM_ALPHA = (2.0 * DEPTH) ** 0.25
LAMBDA_INIT = 0.8 - 0.6 * 1.0

VMEM_LIMIT_V7X = 56 * 1024 * 1024

TOKEN_TILE = 256
ATTN_Q_TILE = 512
ATTN_K_CHUNK = 512
ONES_ROWS = 16
LOG2_E = 1.4426950408889634
EXPERT_ROWS = 256
DISPATCH_TILE = 256
COMBINE_TILE = 256


def _params(n_axes, **kw):
    return pltpu.CompilerParams(dimension_semantics=("arbitrary",) * n_axes,
                                vmem_limit_bytes=VMEM_LIMIT_V7X, **kw)


def _ln_plain(x):
    mu = jnp.mean(x, axis=-1, keepdims=True)
    xc = x - mu
    var = jnp.mean(xc * xc, axis=-1, keepdims=True)
    return xc * lax.rsqrt(var + LN_EPS)


def _dot(a, b):
    return jnp.dot(a, b, preferred_element_type=F32)


def _dot_nt(a, b, **kw):
    return lax.dot_general(a, b, (((1,), (1,)), ((), ())), preferred_element_type=F32, **kw)


def _mod_kernel(c_ref, w_ref, b_ref, o_ref):
    c = c_ref[...]
    s = c * jax.nn.sigmoid(c)
    o_ref[...] = jnp.dot(s, w_ref[...], preferred_element_type=F32,
                         precision=lax.Precision.HIGHEST) + b_ref[...]


def _mod_call(c, w_ada, b_ada):
    rows, d = c.shape
    n_out = w_ada.shape[1]
    return pl.pallas_call(
        _mod_kernel,
        grid=(n_out // d,),
        in_specs=[pl.BlockSpec((rows, d), lambda j: (0, 0)),
                  pl.BlockSpec((d, d), lambda j: (0, j)),
                  pl.BlockSpec((1, d), lambda j: (0, j))],
        out_specs=pl.BlockSpec((rows, d), lambda j: (0, j)),
        out_shape=jax.ShapeDtypeStruct((rows, n_out), F32),
        compiler_params=_params(1),
        name="mod",
    )(c, w_ada, b_ada.reshape(1, n_out))


def _inproj_kernel(x_ref, mod_ref, cos_ref, sina_ref, sinb_ref, wcat_ref, wvt_ref,
                   ua_ref, q_ref, k_ref, vt_ref, sga_ref, sgb_ref):
    d = D_MODEL
    x = x_ref[0]
    sh1 = mod_ref[0, 0:1, :]
    sc1 = mod_ref[0, 1:2, :]
    u = (_ln_plain(x) * (1.0 + sc1) + sh1).astype(BF16)

    def proj(j):
        return _dot(u, wcat_ref[:, j * d:(j + 1) * d])

    reps = d // HEAD_W
    cos = jnp.concatenate([cos_ref[...]] * reps, axis=1)
    sina = jnp.concatenate([sina_ref[...]] * reps, axis=1)
    sinb = jnp.concatenate([sinb_ref[...]] * reps, axis=1)

    def rope(t):
        half = HEAD_DIM // 2
        return t * cos + pltpu.roll(t, d - half, 1) * sina + pltpu.roll(t, half, 1) * sinb

    ua_ref[0] = proj(0)
    q_ref[0] = (rope(proj(1)) * (HEAD_DIM ** -0.5 * LOG2_E)).astype(BF16)
    k_ref[0] = rope(proj(2)).astype(BF16)
    vt_ref[0] = _dot_nt(wvt_ref[...], u).astype(BF16)
    sga_ref[0] = jax.nn.sigmoid(proj(3))
    sgb_ref[0] = jax.nn.sigmoid(proj(4))


def _inproj_call(x, mod, cos, sina, sinb, wcat, wvt):
    b, s, d = x.shape
    tm = min(TOKEN_TILE, s)
    row = lambda bi, i: (bi, i, 0)
    const2 = lambda bi, i: (0, 0)
    tab = pl.BlockSpec((tm, HEAD_W), lambda bi, i: (i, 0))
    return pl.pallas_call(
        _inproj_kernel,
        grid=(b, s // tm),
        in_specs=[pl.BlockSpec((1, tm, d), row),
                  pl.BlockSpec((1, 6, d), lambda bi, i: (bi, 0, 0)),
                  tab, tab, tab,
                  pl.BlockSpec(wcat.shape, const2),
                  pl.BlockSpec(wvt.shape, const2)],
        out_specs=[pl.BlockSpec((1, tm, d), row),
                   pl.BlockSpec((1, tm, d), row),
                   pl.BlockSpec((1, tm, d), row),
                   pl.BlockSpec((1, d, tm), lambda bi, i: (bi, 0, i)),
                   pl.BlockSpec((1, tm, d), row),
                   pl.BlockSpec((1, tm, d), row)],
        out_shape=[jax.ShapeDtypeStruct((b, s, d), F32),
                   jax.ShapeDtypeStruct((b, s, d), BF16),
                   jax.ShapeDtypeStruct((b, s, d), BF16),
                   jax.ShapeDtypeStruct((b, d, s), BF16),
                   jax.ShapeDtypeStruct((b, s, d), F32),
                   jax.ShapeDtypeStruct((b, s, d), F32)],
        compiler_params=_params(2),
        name="inproj",
    )(x, mod, cos, sina, sinb, wcat, wvt)


def _attn_kernel(q_ref, k_ref, vt_ref, lq1_ref, lk1_ref, lq2_ref, lk2_ref, subw_ref,
                 o_ref, sa_ref, sb_ref, o1_ref, o2_ref, *, k_chunk):
    tq = q_ref.shape[1]
    s_len = k_ref.shape[1]
    q = q_ref[0]
    lane = lax.broadcasted_iota(jnp.int32, q.shape, 1)
    zero = jnp.zeros_like(q)
    q1 = jnp.where(lane < HEAD_DIM, q, zero)
    q2 = jnp.where(lane >= HEAD_DIM, q, zero)
    o1_ref[...] = jnp.zeros_like(o1_ref)
    o2_ref[...] = jnp.zeros_like(o2_ref)
    n_chunks = s_len // k_chunk
    ones = jnp.ones((ONES_ROWS, k_chunk), BF16)

    def scores(c, s_ref):
        start = pl.multiple_of(c * k_chunk, k_chunk)
        kc = k_ref[0, pl.ds(start, k_chunk), :]
        s1 = _dot_nt(kc, q1)
        s2 = _dot_nt(kc, q2)
        s_ref[0] = s1
        s_ref[1] = s2
        return jnp.max(s1, axis=0, keepdims=True), jnp.max(s2, axis=0, keepdims=True)

    def accumulate(c, s_ref, cmax, m):
        start = pl.multiple_of(c * k_chunk, k_chunk)
        vtc = jnp.concatenate([vt_ref[0, :, pl.ds(start, k_chunk)], ones], axis=0)
        m_out = []
        for j, acc_ref in enumerate((o1_ref, o2_ref)):
            m_new = jnp.maximum(m[j], cmax[j])
            alpha = jnp.exp2(m[j] - m_new)
            p = jnp.exp2(s_ref[j] - m_new).astype(BF16)
            acc_ref[...] = alpha * acc_ref[...] + _dot(vtc, p)
            m_out.append(m_new)
        return tuple(m_out)

    def body(i, carry):
        cmax_a, m = carry
        cmax_b = scores(2 * i + 1, sb_ref)
        m = accumulate(2 * i, sa_ref, cmax_a, m)
        cmax_a = scores(2 * i + 2, sa_ref)
        m = accumulate(2 * i + 1, sb_ref, cmax_b, m)
        return cmax_a, m

    neg = jnp.full((1, tq), -jnp.inf, F32)
    cmax_a, m = lax.fori_loop(0, n_chunks // 2 - 1, body, (scores(0, sa_ref), (neg, neg)))
    cmax_b = scores(n_chunks - 1, sb_ref)
    m = accumulate(n_chunks - 2, sa_ref, cmax_a, m)
    accumulate(n_chunks - 1, sb_ref, cmax_b, m)

    lam = (jnp.exp(jnp.sum(lq1_ref[...] * lk1_ref[...], axis=1, keepdims=True))
           - jnp.exp(jnp.sum(lq2_ref[...] * lk2_ref[...], axis=1, keepdims=True))
           + LAMBDA_INIT)
    acc1 = o1_ref[...]
    acc2 = o2_ref[...]
    o = (acc1[:HEAD_W] / acc1[HEAD_W:HEAD_W + 1]
         - lam * (acc2[:HEAD_W] / acc2[HEAD_W:HEAD_W + 1]))
    o = o * lax.rsqrt(jnp.mean(o * o, axis=0, keepdims=True) + LN_EPS)
    o = o * subw_ref[...] * (1.0 - LAMBDA_INIT)
    o_ref[0] = o.T.astype(o_ref.dtype)


def _attn_call(q, k, vt, lq1, lk1, lq2, lk2, subw):
    b, s, d = q.shape
    tq = min(ATTN_Q_TILE, s)
    kc = min(ATTN_K_CHUNK, s // 2)
    assert s % (2 * kc) == 0 and s % tq == 0
    lam_spec = pl.BlockSpec((1, HEAD_DIM), lambda bi, h, i: (0, 0))
    return pl.pallas_call(
        functools.partial(_attn_kernel, k_chunk=kc),
        grid=(b, N_HEADS, s // tq),
        in_specs=[pl.BlockSpec((1, tq, HEAD_W), lambda bi, h, i: (bi, i, h)),
                  pl.BlockSpec((1, s, HEAD_W), lambda bi, h, i: (bi, 0, h)),
                  pl.BlockSpec((1, HEAD_W, s), lambda bi, h, i: (bi, h, 0)),
                  lam_spec, lam_spec, lam_spec, lam_spec,
                  pl.BlockSpec((HEAD_W, 1), lambda bi, h, i: (0, 0))],
        out_specs=pl.BlockSpec((1, tq, HEAD_W), lambda bi, h, i: (bi, i, h)),
        out_shape=jax.ShapeDtypeStruct((b, s, d), BF16),
        scratch_shapes=[pltpu.VMEM((2, kc, tq), F32), pltpu.VMEM((2, kc, tq), F32),
                        pltpu.VMEM((HEAD_W + ONES_ROWS, tq), F32),
                        pltpu.VMEM((HEAD_W + ONES_ROWS, tq), F32)],
        compiler_params=_params(3),
        name="attn",
    )(q, k, vt, lq1, lk1, lq2, lk2, subw)


def _mixer_kernel(x_ref, ua_ref, uprev_ref, unext_ref, sga_ref, sgb_ref, yattn_ref, mod_ref,
                  poolw_ref, pscale_ref, wa_ref, wb_ref, wout_ref, g_ref, b_ref, rwt_ref, rb_ref,
                  x1_ref, eidx_ref, rank_ref, gate_ref, cnt_ref,
                  ext_ref, run_ref, *, seq_len):
    tm = x_ref.shape[1]
    halo = POOL_HALO
    bi = pl.program_id(0)
    i = pl.program_id(1)
    n_i = pl.num_programs(1)

    @pl.when((bi == 0) & (i == 0))
    def _():
        run_ref[...] = jnp.zeros_like(run_ref)

    ua = ua_ref[0]
    ext_ref[0:halo, :] = jnp.where(i > 0, uprev_ref[0], 0.0)
    ext_ref[halo:halo + tm, :] = ua
    ext_ref[halo + tm:, :] = jnp.where(i < n_i - 1, unext_ref[0], 0.0)
    pos = i * tm + lax.broadcasted_iota(jnp.int32, (tm, 1), 0)
    mixed = []
    for g, w in enumerate(POOL_WINDOWS):
        cols = slice(g * POOL_GC, (g + 1) * POOL_GC)
        win = ext_ref[halo - w // 2:halo - w // 2 + tm, cols]
        for j in range(1, w):
            off = halo - w // 2 + j
            win = win + ext_ref[off:off + tm, cols]
        lo = jnp.maximum(pos - w // 2, 0)
        hi = jnp.minimum(pos + w // 2 - 1, seq_len - 1)
        cnt = (hi - lo + 1).astype(F32)
        pooled = win / cnt - ua[:, cols]
        mixed.append(_dot(pooled.astype(BF16), poolw_ref[g]))
    mixed = jnp.concatenate(mixed, axis=1) * pscale_ref[...]
    ya = _dot(mixed.astype(BF16), wa_ref[...])

    yb = _dot(yattn_ref[0], wb_ref[...])
    merged = sga_ref[0] * ya + sgb_ref[0] * yb
    h = _dot(merged.astype(BF16), wout_ref[...])
    g1 = mod_ref[0, 2:3, :]
    x1 = _ln_plain(DEEPNORM_ALPHA * x_ref[0] + g1 * h) * g_ref[...] + b_ref[...]
    x1_ref[0] = x1

    sh2 = mod_ref[0, 3:4, :]
    sc2 = mod_ref[0, 4:5, :]
    u2 = _ln_plain(x1) * (1.0 + sc2) + sh2
    logits = _dot_nt(rwt_ref[...], u2, precision=lax.Precision.HIGHEST) + rb_ref[...]
    eio = lax.broadcasted_iota(jnp.int32, logits.shape, 0)
    vals, idxs, sels = [], [], []
    l = logits
    for _ in range(TOP_K):
        mv = jnp.max(l, axis=0, keepdims=True)
        idx = jnp.min(jnp.where(l == mv, eio, N_EXPERTS), axis=0, keepdims=True)
        sel = eio == idx
        vals.append(mv)
        idxs.append(idx)
        sels.append(sel)
        l = jnp.where(sel, -jnp.inf, l)
    ex = [jnp.exp(v - vals[0]) for v in vals]
    den = ex[0] + ex[1] + ex[2] + ex[3]
    gate_ref[...] = jnp.concatenate([e / den for e in ex], axis=0)
    eidx_ref[...] = jnp.concatenate(idxs, axis=0)

    member = jnp.zeros(logits.shape, F32)
    for sel in sels:
        member = member + sel.astype(F32)
    tri = (lax.broadcasted_iota(jnp.int32, (tm, tm), 0)
           < lax.broadcasted_iota(jnp.int32, (tm, tm), 1)).astype(BF16)
    base = run_ref[...] + _dot(member.astype(BF16), tri)
    ranks = [jnp.sum(jnp.where(sel, base, 0.0), axis=0, keepdims=True) for sel in sels]
    rank_ref[...] = jnp.concatenate(ranks, axis=0).astype(jnp.int32)
    run_new = run_ref[...] + jnp.sum(member, axis=1, keepdims=True)
    run_ref[...] = run_new
    cnt_ref[...] = jnp.broadcast_to(run_new, cnt_ref.shape)


def _mixer_call(x, ua, sga, sgb, yattn, mod, poolw, pscale, wa, wb, wout, g, b_, rwt, rb):
    b, s, d = x.shape
    tm = min(TOKEN_TILE, s)
    n_i = s // tm
    hb = tm // POOL_HALO
    n_hb = s // POOL_HALO
    row = lambda bi, i: (bi, i, 0)
    c2 = lambda bi, i: (0, 0)
    c3 = lambda bi, i: (0, 0, 0)
    tokcol = lambda bi, i: (0, bi * n_i + i)
    n_tok = b * s
    return pl.pallas_call(
        functools.partial(_mixer_kernel, seq_len=s),
        grid=(b, n_i),
        in_specs=[pl.BlockSpec((1, tm, d), row),
                  pl.BlockSpec((1, tm, d), row),
                  pl.BlockSpec((1, POOL_HALO, d), lambda bi, i: (bi, jnp.maximum(i * hb - 1, 0), 0)),
                  pl.BlockSpec((1, POOL_HALO, d), lambda bi, i: (bi, jnp.minimum((i + 1) * hb, n_hb - 1), 0)),
                  pl.BlockSpec((1, tm, d), row),
                  pl.BlockSpec((1, tm, d), row),
                  pl.BlockSpec((1, tm, d), row),
                  pl.BlockSpec((1, 6, d), lambda bi, i: (bi, 0, 0)),
                  pl.BlockSpec(poolw.shape, c3),
                  pl.BlockSpec((1, d), c2),
                  pl.BlockSpec((d, d), c2),
                  pl.BlockSpec((d, d), c2),
                  pl.BlockSpec((d, d), c2),
                  pl.BlockSpec((1, d), c2),
                  pl.BlockSpec((1, d), c2),
                  pl.BlockSpec((N_EXPERTS, d), c2),
                  pl.BlockSpec((N_EXPERTS, 1), c2)],
        out_specs=[pl.BlockSpec((1, tm, d), row),
                   pl.BlockSpec((TOP_K, tm), tokcol),
                   pl.BlockSpec((TOP_K, tm), tokcol),
                   pl.BlockSpec((TOP_K, tm), tokcol),
                   pl.BlockSpec((N_EXPERTS, 128), c2)],
        out_shape=[jax.ShapeDtypeStruct((b, s, d), F32),
                   jax.ShapeDtypeStruct((TOP_K, n_tok), jnp.int32),
                   jax.ShapeDtypeStruct((TOP_K, n_tok), jnp.int32),
                   jax.ShapeDtypeStruct((TOP_K, n_tok), F32),
                   jax.ShapeDtypeStruct((N_EXPERTS, 128), F32)],
        scratch_shapes=[pltpu.VMEM((tm + 2 * POOL_HALO, d), F32),
                        pltpu.VMEM((N_EXPERTS, 1), F32)],
        compiler_params=_params(2),
        name="mixer",
    )(x, ua, ua, ua, sga, sgb, yattn, mod, poolw, pscale, wa, wb, wout, g, b_, rwt, rb)


def _row_copy(src_ref, src_row, dst_ref, dst_row, sem):
    return pltpu.make_async_copy(src_ref.at[pl.ds(src_row, 1)], dst_ref.at[pl.ds(dst_row, 1)], sem)


def _dispatch_kernel(dest_ref, x1_ref, mod_ref, xb_in_ref, xb_ref, u_ref, sem):
    del xb_in_ref
    td = x1_ref.shape[0]
    sh2 = mod_ref[0, 3:4, :]
    sc2 = mod_ref[0, 4:5, :]
    u_ref[...] = _ln_plain(x1_ref[...]) * (1.0 + sc2) + sh2

    def issue(t, c):
        for k in range(TOP_K):
            _row_copy(u_ref, t, xb_ref, dest_ref[k, t], sem).start()
        return c

    lax.fori_loop(0, td, issue, 0)

    def drain(t, c):
        for k in range(TOP_K):
            _row_copy(u_ref, 0, xb_ref, 0, sem).wait()
        return c

    lax.fori_loop(0, td, drain, 0)


def _dispatch_call(dest, x1_flat, mod, xb_zero, seq_len):
    n_tok, d = x1_flat.shape
    td = min(DISPATCH_TILE, seq_len)
    return pl.pallas_call(
        _dispatch_kernel,
        grid=(n_tok // td,),
        in_specs=[pl.BlockSpec((TOP_K, td), lambda i: (0, i), memory_space=pltpu.SMEM),
                  pl.BlockSpec((td, d), lambda i: (i, 0)),
                  pl.BlockSpec((1, 6, d), lambda i: ((i * td) // seq_len, 0, 0)),
                  pl.BlockSpec(memory_space=pl.ANY)],
        out_specs=pl.BlockSpec(memory_space=pl.ANY),
        out_shape=jax.ShapeDtypeStruct(xb_zero.shape, F32),
        scratch_shapes=[pltpu.VMEM((td, d), F32), pltpu.SemaphoreType.DMA],
        input_output_aliases={3: 0},
        compiler_params=_params(1, has_side_effects=True),
        name="dispatch",
    )(dest, x1_flat, mod, xb_zero)


def _expert_kernel(blk_e_ref, n_used_ref, xb_ref, wup_ref, bup_ref, wdn_ref, bdn_ref, yb_ref):
    del blk_e_ref
    used = pl.program_id(0) < n_used_ref[0]

    @pl.when(jnp.logical_not(used))
    def _():
        yb_ref[...] = jnp.zeros_like(yb_ref)

    @pl.when(used)
    def _():
        x = xb_ref[...].astype(BF16)
        h = _dot(x, wup_ref[0]) + bup_ref[0]
        gate = jnp.minimum(h[:, :D_FF], SWIGLU_LIMIT)
        up = jnp.clip(h[:, D_FF:], -SWIGLU_LIMIT, SWIGLU_LIMIT)
        act = (up + 1.0) * (gate * jax.nn.sigmoid(SWIGLU_ALPHA * gate))
        yb_ref[...] = _dot(act.astype(BF16), wdn_ref[0]) + bdn_ref[0]


def _expert_call(blk_e, n_used, xb, wup, bup, wdn, bdn):
    n_rows, d = xb.shape
    rows = EXPERT_ROWS
    last = lambda r, be, nu: jnp.minimum(r, nu[0] - 1)
    rowblk = lambda r, be, nu: (last(r, be, nu), 0)
    exp3 = lambda r, be, nu: (be[last(r, be, nu)], 0, 0)
    return pl.pallas_call(
        _expert_kernel,
        grid_spec=pltpu.PrefetchScalarGridSpec(
            num_scalar_prefetch=2,
            grid=(n_rows // rows,),
            in_specs=[pl.BlockSpec((rows, d), rowblk),
                      pl.BlockSpec((1, d, 2 * D_FF), exp3),
                      pl.BlockSpec((1, 1, 2 * D_FF), exp3),
                      pl.BlockSpec((1, D_FF, d), exp3),
                      pl.BlockSpec((1, 1, d), exp3)],
            out_specs=pl.BlockSpec((rows, d), lambda r, be, nu: (r, 0))),
        out_shape=jax.ShapeDtypeStruct((n_rows, d), F32),
        compiler_params=_params(1),
        name="experts",
    )(blk_e, n_used, xb, wup, bup, wdn, bdn)


def _combine_kernel(dest_ref, x1_ref, gates_ref, mod_ref, g_ref, b_ref, yb_ref, o_ref, buf_ref, sem):
    tc = x1_ref.shape[0]

    def issue(t, c):
        for k in range(TOP_K):
            _row_copy(yb_ref, dest_ref[k, t], buf_ref.at[k], t, sem).start()
        return c

    lax.fori_loop(0, tc, issue, 0)

    def drain(t, c):
        for k in range(TOP_K):
            _row_copy(yb_ref, 0, buf_ref.at[k], 0, sem).wait()
        return c

    lax.fori_loop(0, tc, drain, 0)

    gates = gates_ref[...]
    h2 = gates[:, 0:1] * buf_ref[0]
    for k in range(1, TOP_K):
        h2 = h2 + gates[:, k:k + 1] * buf_ref[k]
    g2 = mod_ref[0, 5:6, :]
    o_ref[...] = _ln_plain(DEEPNORM_ALPHA * x1_ref[...] + g2 * h2) * g_ref[...] + b_ref[...]


def _combine_call(dest, x1_flat, gates_t, mod, g, b_, yb, seq_len):
    n_tok, d = x1_flat.shape
    tc = min(COMBINE_TILE, seq_len)
    c2 = lambda i: (0, 0)
    return pl.pallas_call(
        _combine_kernel,
        grid=(n_tok // tc,),
        in_specs=[pl.BlockSpec((TOP_K, tc), lambda i: (0, i), memory_space=pltpu.SMEM),
                  pl.BlockSpec((tc, d), lambda i: (i, 0)),
                  pl.BlockSpec((tc, TOP_K), lambda i: (i, 0)),
                  pl.BlockSpec((1, 6, d), lambda i: ((i * tc) // seq_len, 0, 0)),
                  pl.BlockSpec((1, d), c2),
                  pl.BlockSpec((1, d), c2),
                  pl.BlockSpec(memory_space=pl.ANY)],
        out_specs=pl.BlockSpec((tc, d), lambda i: (i, 0)),
        out_shape=jax.ShapeDtypeStruct((n_tok, d), F32),
        scratch_shapes=[pltpu.VMEM((TOP_K, tc, d), F32), pltpu.SemaphoreType.DMA],
        compiler_params=_params(1),
        name="combine",
    )(dest, x1_flat, gates_t, mod, g, b_, yb)


def _rope_tables(seq_len):
    inv_freq = ROPE_THETA ** (-jnp.arange(0, HEAD_DIM, 2, dtype=F32) / HEAD_DIM)
    ang = jnp.arange(seq_len, dtype=F32)[:, None] * inv_freq[None, :]
    cos, sin = lax.optimization_barrier((jnp.cos(ang), jnp.sin(ang)))
    zero = jnp.zeros_like(sin)
    reps = HEAD_W // HEAD_DIM
    cos_t = jnp.tile(jnp.concatenate([cos, cos], axis=1), (1, reps))
    sina_t = jnp.tile(jnp.concatenate([-sin, zero], axis=1), (1, reps))
    sinb_t = jnp.tile(jnp.concatenate([zero, sin], axis=1), (1, reps))
    return cos_t, sina_t, sinb_t


def _encode(x, mod, w):
    b, s, d = x.shape
    n_tok = b * s
    cos, sina, sinb = w["rope"]
    ua, q, k, vt, sga, sgb = _inproj_call(x, mod, cos, sina, sinb, w["wcat"], w["wvt"])
    yattn = _attn_call(q, k, vt, w["lq1"], w["lk1"], w["lq2"], w["lk2"], w["subw"])
    x1, eidx, rank, gates, cnt = _mixer_call(
        x, ua, sga, sgb, yattn, mod, w["poolw"], w["pscale"], w["wa"], w["wb"], w["wout"],
        w["ln1_g"], w["ln1_b"], w["rwt"], w["rb"])

    rows = EXPERT_ROWS
    counts = cnt[:, 0].astype(jnp.int32)
    padded = (counts + rows - 1) // rows * rows
    pends = jnp.cumsum(padded)
    pstarts = pends - padded
    e_ids = jnp.arange(N_EXPERTS, dtype=jnp.int32)
    dest = rank + jnp.sum(jnp.where(eidx[None] == e_ids[:, None, None], pstarts[:, None, None], 0), axis=0)
    n_blocks = (n_tok * TOP_K) // rows + N_EXPERTS
    blk_row0 = jnp.arange(n_blocks, dtype=jnp.int32) * rows
    blk_e = jnp.minimum(jnp.sum((pends[None, :] <= blk_row0[:, None]).astype(jnp.int32), axis=1),
                        N_EXPERTS - 1)
    n_used = (pends[-1:] // rows).astype(jnp.int32)

    x1_flat = x1.reshape(n_tok, d)
    xb = _dispatch_call(dest, x1_flat, mod, jnp.zeros((n_blocks * rows, d), F32), s)
    yb = _expert_call(blk_e, n_used, xb, w["wup"], w["bup"], w["wdn"], w["bdn"])
    out = _combine_call(dest, x1_flat, gates.T, mod, w["ln2_g"], w["ln2_b"], yb, s)
    return out.reshape(b, s, d)


def kernel(x_prompt, x_sample, c_prompt, c_sample, w_ada, b_ada, w_in, pool_w, pool_scale, w_branch_a, w_branch_b, lambda_q1, lambda_k1, lambda_q2, lambda_k2, subln_w, w_out, ln1_g, ln1_b, router_w, router_b, exp_w_up, exp_b_up, exp_w_down, exp_b_down, ln2_g, ln2_b):
    d = D_MODEL
    w_in0 = w_in[0]
    seg = lambda j: w_in0[:, j * d:(j + 1) * d]
    w = {
        "wcat": jnp.concatenate([seg(0), seg(1), seg(2), seg(4), seg(5)], axis=1).astype(BF16),
        "wvt": seg(3).T.astype(BF16),
        "lq1": lambda_q1, "lk1": lambda_k1, "lq2": lambda_q2, "lk2": lambda_k2,
        "subw": subln_w[0].reshape(HEAD_W, 1),
        "poolw": pool_w[0].astype(BF16),
        "pscale": pool_scale,
        "wa": w_branch_a[0].astype(BF16),
        "wb": w_branch_b[0].astype(BF16),
        "wout": w_out[0].astype(BF16),
        "ln1_g": ln1_g, "ln1_b": ln1_b,
        "rwt": router_w[0].T,
        "rb": router_b[0].reshape(N_EXPERTS, 1),
        "wup": exp_w_up[0].astype(BF16),
        "bup": exp_b_up[0].reshape(N_EXPERTS, 1, 2 * D_FF),
        "wdn": exp_w_down[0].astype(BF16),
        "bdn": exp_b_down[0].reshape(N_EXPERTS, 1, d),
        "ln2_g": ln2_g, "ln2_b": ln2_b,
        "rope": _rope_tables(max(x_prompt.shape[1], x_sample.shape[1])),
    }
    nb_p, nb_s = c_prompt.shape[0], c_sample.shape[0]
    pad = (-(nb_p + nb_s)) % 8
    c_all = jnp.concatenate([c_prompt, c_sample, jnp.zeros((pad, d), F32)], axis=0)
    mod = _mod_call(c_all, w_ada[0], b_ada[0]).reshape(-1, 6, d)
    y_prompt = _encode(x_prompt, mod[:nb_p], w)
    y_sample = _encode(x_sample, mod[nb_p:nb_p + nb_s], w)
    return (y_prompt, y_sample)
```

```python
import functools

import jax
import jax.numpy as jnp
from jax import lax
from jax.experimental import pallas as pl
from jax.experimental.pallas import tpu as pltpu

F32 = jnp.float32
BF16 = jnp.bfloat16

D_MODEL = 1024
N_HEADS = 8
HEAD_DIM = 64
HEAD_W = 2 * HEAD_DIM
ROPE_THETA = 10000.0
POOL_WINDOWS = (2, 4, 8, 16)
POOL_GC = 256
POOL_HALO = 8
N_EXPERTS = 32
TOP_K = 4
D_FF = 1024
SWIGLU_LIMIT = 7.0
SWIGLU_ALPHA = 1.702
LN_EPS = 1e-5
DEPTH = 1
DEEPNORM_ALPHA = (2.0 * DEPTH) ** 0.25
LAMBDA_INIT = 0.8 - 0.6 * 1.0

VMEM_LIMIT_V7X = 56 * 1024 * 1024

TOKEN_TILE = 256
MOE_TILE = TOKEN_TILE
ATTN_Q_TILE = 512
ATTN_K_CHUNK = 512
ONES_ROWS = 16
LOG2_E = 1.4426950408889634
EXPERT_ROWS = 256


def _params(n_axes, **kw):
    return pltpu.CompilerParams(dimension_semantics=("arbitrary",) * n_axes,
                                vmem_limit_bytes=VMEM_LIMIT_V7X, **kw)


def _ln_plain(x):
    mu = jnp.mean(x, axis=-1, keepdims=True)
    xc = x - mu
    var = jnp.mean(xc * xc, axis=-1, keepdims=True)
    return xc * lax.rsqrt(var + LN_EPS)


def _dot(a, b):
    return jnp.dot(a, b, preferred_element_type=F32)


def _dot_nt(a, b, **kw):
    return lax.dot_general(a, b, (((1,), (1,)), ((), ())), preferred_element_type=F32, **kw)


def _mod_kernel(c_ref, w_ref, b_ref, o_ref):
    c = c_ref[...]
    s = c * jax.nn.sigmoid(c)
    o_ref[...] = jnp.dot(s, w_ref[...], preferred_element_type=F32,
                         precision=lax.Precision.HIGHEST) + b_ref[...]


def _mod_call(c, w_ada, b_ada):
    rows, d = c.shape
    n_out = w_ada.shape[1]
    return pl.pallas_call(
        _mod_kernel,
        grid=(n_out // d,),
        in_specs=[pl.BlockSpec((rows, d), lambda j: (0, 0)),
                  pl.BlockSpec((d, d), lambda j: (0, j)),
                  pl.BlockSpec((1, d), lambda j: (0, j))],
        out_specs=pl.BlockSpec((rows, d), lambda j: (0, j)),
        out_shape=jax.ShapeDtypeStruct((rows, n_out), F32),
        compiler_params=_params(1),
        name="mod",
    )(c, w_ada, b_ada.reshape(1, n_out))


def _inproj_kernel(x_ref, mod_ref, cos_ref, sina_ref, sinb_ref, wcat_ref, wvt_ref,
                   ua_ref, q_ref, k_ref, vt_ref, sga_ref, sgb_ref):
    d = D_MODEL
    x = x_ref[0]
    sh1 = mod_ref[0, 0:1, :]
    sc1 = mod_ref[0, 1:2, :]
    u = (_ln_plain(x) * (1.0 + sc1) + sh1).astype(BF16)

    def proj(j):
        return _dot(u, wcat_ref[:, j * d:(j + 1) * d])

    reps = d // HEAD_W
    cos = jnp.concatenate([cos_ref[...]] * reps, axis=1)
    sina = jnp.concatenate([sina_ref[...]] * reps, axis=1)
    sinb = jnp.concatenate([sinb_ref[...]] * reps, axis=1)

    def rope(t):
        half = HEAD_DIM // 2
        return t * cos + pltpu.roll(t, d - half, 1) * sina + pltpu.roll(t, half, 1) * sinb

    ua_ref[0] = proj(0)
    q_ref[0] = (rope(proj(1)) * (HEAD_DIM ** -0.5 * LOG2_E)).astype(BF16)
    k_ref[0] = rope(proj(2)).astype(BF16)
    vt_ref[0] = _dot_nt(wvt_ref[...], u).astype(BF16)
    sga_ref[0] = jax.nn.sigmoid(proj(3))
    sgb_ref[0] = jax.nn.sigmoid(proj(4))


def _inproj_call(x, mod, cos, sina, sinb, wcat, wvt):
    b, s, d = x.shape
    tm = min(TOKEN_TILE, s)
    row = lambda bi, i: (bi, i, 0)
    const2 = lambda bi, i: (0, 0)
    tab = pl.BlockSpec((tm, HEAD_W), lambda bi, i: (i, 0))
    return pl.pallas_call(
        _inproj_kernel,
        grid=(b, s // tm),
        in_specs=[pl.BlockSpec((1, tm, d), row),
                  pl.BlockSpec((1, 6, d), lambda bi, i: (bi, 0, 0)),
                  tab, tab, tab,
                  pl.BlockSpec(wcat.shape, const2),
                  pl.BlockSpec(wvt.shape, const2)],
        out_specs=[pl.BlockSpec((1, tm, d), row),
                   pl.BlockSpec((1, tm, d), row),
                   pl.BlockSpec((1, tm, d), row),
                   pl.BlockSpec((1, d, tm), lambda bi, i: (bi, 0, i)),
                   pl.BlockSpec((1, tm, d), row),
                   pl.BlockSpec((1, tm, d), row)],
        out_shape=[jax.ShapeDtypeStruct((b, s, d), F32),
                   jax.ShapeDtypeStruct((b, s, d), BF16),
                   jax.ShapeDtypeStruct((b, s, d), BF16),
                   jax.ShapeDtypeStruct((b, d, s), BF16),
                   jax.ShapeDtypeStruct((b, s, d), F32),
                   jax.ShapeDtypeStruct((b, s, d), F32)],
        compiler_params=_params(2),
        name="inproj",
    )(x, mod, cos, sina, sinb, wcat, wvt)


def _attn_kernel(q_ref, k_ref, vt_ref, lq1_ref, lk1_ref, lq2_ref, lk2_ref, subw_ref,
                 o_ref, sa_ref, sb_ref, o1_ref, o2_ref, *, k_chunk):
    tq = q_ref.shape[1]
    s_len = k_ref.shape[1]
    q = q_ref[0]
    lane = lax.broadcasted_iota(jnp.int32, q.shape, 1)
    zero = jnp.zeros_like(q)
    q1 = jnp.where(lane < HEAD_DIM, q, zero)
    q2 = jnp.where(lane >= HEAD_DIM, q, zero)
    o1_ref[...] = jnp.zeros_like(o1_ref)
    o2_ref[...] = jnp.zeros_like(o2_ref)
    n_chunks = s_len // k_chunk
    ones = jnp.ones((ONES_ROWS, k_chunk), BF16)

    def scores(c, s_ref):
        start = pl.multiple_of(c * k_chunk, k_chunk)
        kc = k_ref[0, pl.ds(start, k_chunk), :]
        s1 = _dot_nt(kc, q1)
        s2 = _dot_nt(kc, q2)
        s_ref[0] = s1
        s_ref[1] = s2
        return jnp.max(s1, axis=0, keepdims=True), jnp.max(s2, axis=0, keepdims=True)

    def accumulate(c, s_ref, cmax, m):
        start = pl.multiple_of(c * k_chunk, k_chunk)
        vtc = jnp.concatenate([vt_ref[0, :, pl.ds(start, k_chunk)], ones], axis=0)
        m_out = []
        for j, acc_ref in enumerate((o1_ref, o2_ref)):
            m_new = jnp.maximum(m[j], cmax[j])
            alpha = jnp.exp2(m[j] - m_new)
            p = jnp.exp2(s_ref[j] - m_new).astype(BF16)
            acc_ref[...] = alpha * acc_ref[...] + _dot(vtc, p)
            m_out.append(m_new)
        return tuple(m_out)

    def body(i, carry):
        cmax_a, m = carry
        cmax_b = scores(2 * i + 1, sb_ref)
        m = accumulate(2 * i, sa_ref, cmax_a, m)
        cmax_a = scores(2 * i + 2, sa_ref)
        m = accumulate(2 * i + 1, sb_ref, cmax_b, m)
        return cmax_a, m

    neg = jnp.full((1, tq), -jnp.inf, F32)
    cmax_a, m = lax.fori_loop(0, n_chunks // 2 - 1, body, (scores(0, sa_ref), (neg, neg)))
    cmax_b = scores(n_chunks - 1, sb_ref)
    m = accumulate(n_chunks - 2, sa_ref, cmax_a, m)
    accumulate(n_chunks - 1, sb_ref, cmax_b, m)

    lam = (jnp.exp(jnp.sum(lq1_ref[...] * lk1_ref[...], axis=1, keepdims=True))
           - jnp.exp(jnp.sum(lq2_ref[...] * lk2_ref[...], axis=1, keepdims=True))
           + LAMBDA_INIT)
    acc1 = o1_ref[...]
    acc2 = o2_ref[...]
    o = (acc1[:HEAD_W] / acc1[HEAD_W:HEAD_W + 1]
         - lam * (acc2[:HEAD_W] / acc2[HEAD_W:HEAD_W + 1]))
    o = o * lax.rsqrt(jnp.mean(o * o, axis=0, keepdims=True) + LN_EPS)
    o = o * subw_ref[...] * (1.0 - LAMBDA_INIT)
    o_ref[0] = o.T.astype(o_ref.dtype)


def _attn_call(q, k, vt, lq1, lk1, lq2, lk2, subw):
    b, s, d = q.shape
    tq = min(ATTN_Q_TILE, s)
    kc = min(ATTN_K_CHUNK, s // 2)
    assert s % (2 * kc) == 0 and s % tq == 0
    lam_spec = pl.BlockSpec((1, HEAD_DIM), lambda bi, h, i: (0, 0))
    return pl.pallas_call(
        functools.partial(_attn_kernel, k_chunk=kc),
        grid=(b, N_HEADS, s // tq),
        in_specs=[pl.BlockSpec((1, tq, HEAD_W), lambda bi, h, i: (bi, i, h)),
                  pl.BlockSpec((1, s, HEAD_W), lambda bi, h, i: (bi, 0, h)),
                  pl.BlockSpec((1, HEAD_W, s), lambda bi, h, i: (bi, h, 0)),
                  lam_spec, lam_spec, lam_spec, lam_spec,
                  pl.BlockSpec((HEAD_W, 1), lambda bi, h, i: (0, 0))],
        out_specs=pl.BlockSpec((1, tq, HEAD_W), lambda bi, h, i: (bi, i, h)),
        out_shape=jax.ShapeDtypeStruct((b, s, d), BF16),
        scratch_shapes=[pltpu.VMEM((2, kc, tq), F32), pltpu.VMEM((2, kc, tq), F32),
                        pltpu.VMEM((HEAD_W + ONES_ROWS, tq), F32),
                        pltpu.VMEM((HEAD_W + ONES_ROWS, tq), F32)],
        compiler_params=_params(3),
        name="attn",
    )(q, k, vt, lq1, lk1, lq2, lk2, subw)


def _mixer_kernel(x_ref, ua_ref, uprev_ref, unext_ref, sga_ref, sgb_ref, yattn_ref, mod_ref,
                  poolw_ref, pscale_ref, wa_ref, wb_ref, wout_ref, g_ref, b_ref, rwt_ref, rb_ref,
                  x1_ref, pos_ref, gate_ref, seg_ref, cnt_ref,
                  ext_ref, run_ref, *, seq_len):
    tm = x_ref.shape[1]
    halo = POOL_HALO
    bi = pl.program_id(0)
    i = pl.program_id(1)
    n_i = pl.num_programs(1)

    @pl.when((bi == 0) & (i == 0))
    def _():
        run_ref[...] = jnp.zeros_like(run_ref)

    ua = ua_ref[0]
    ext_ref[0:halo, :] = jnp.where(i > 0, uprev_ref[0], 0.0)
    ext_ref[halo:halo + tm, :] = ua
    ext_ref[halo + tm:, :] = jnp.where(i < n_i - 1, unext_ref[0], 0.0)
    pos = i * tm + lax.broadcasted_iota(jnp.int32, (tm, 1), 0)
    mixed = []
    for g, w in enumerate(POOL_WINDOWS):
        cols = slice(g * POOL_GC, (g + 1) * POOL_GC)
        win = ext_ref[halo - w // 2:halo - w // 2 + tm, cols]
        for j in range(1, w):
            off = halo - w // 2 + j
            win = win + ext_ref[off:off + tm, cols]
        lo = jnp.maximum(pos - w // 2, 0)
        hi = jnp.minimum(pos + w // 2 - 1, seq_len - 1)
        cnt = (hi - lo + 1).astype(F32)
        pooled = win / cnt - ua[:, cols]
        mixed.append(_dot(pooled.astype(BF16), poolw_ref[g]))
    mixed = jnp.concatenate(mixed, axis=1) * pscale_ref[...]
    ya = _dot(mixed.astype(BF16), wa_ref[...])

    yb = _dot(yattn_ref[0], wb_ref[...])
    merged = sga_ref[0] * ya + sgb_ref[0] * yb
    h = _dot(merged.astype(BF16), wout_ref[...])
    g1 = mod_ref[0, 2:3, :]
    x1 = _ln_plain(DEEPNORM_ALPHA * x_ref[0] + g1 * h) * g_ref[...] + b_ref[...]
    x1_ref[0] = x1

    sh2 = mod_ref[0, 3:4, :]
    sc2 = mod_ref[0, 4:5, :]
    u2 = _ln_plain(x1) * (1.0 + sc2) + sh2
    logits = _dot_nt(rwt_ref[...], u2.astype(BF16)) + rb_ref[...]
    eio = lax.broadcasted_iota(jnp.int32, logits.shape, 0)
    vals, sels = [], []
    l = logits
    for _ in range(TOP_K):
        mv = jnp.max(l, axis=0, keepdims=True)
        idx = jnp.min(jnp.where(l == mv, eio, N_EXPERTS), axis=0, keepdims=True)
        sel = eio == idx
        vals.append(mv)
        sels.append(sel)
        l = jnp.where(sel, -jnp.inf, l)
    ex = [jnp.exp(v - vals[0]) for v in vals]
    den = ex[0] + ex[1] + ex[2] + ex[3]
    gate_ref[...] = jnp.concatenate([e / den for e in ex], axis=0)

    member = jnp.zeros(logits.shape, F32)
    for sel in sels:
        member = member + sel.astype(F32)
    member_b = member.astype(BF16)
    tok_before = (lax.broadcasted_iota(jnp.int32, (tm, tm), 0)
                  < lax.broadcasted_iota(jnp.int32, (tm, tm), 1)).astype(BF16)
    exp_before = (lax.broadcasted_iota(jnp.int32, (N_EXPERTS, N_EXPERTS), 1)
                  < lax.broadcasted_iota(jnp.int32, (N_EXPERTS, N_EXPERTS), 0)).astype(BF16)
    rank_in_tile = _dot(member_b, tok_before)
    seg_cnt = jnp.sum(member, axis=1, keepdims=True)
    seg_rows = jnp.ceil(seg_cnt * (1.0 / SEG_ALIGN)) * SEG_ALIGN
    seg_rows_b = jnp.broadcast_to(seg_rows, (N_EXPERTS, 128)).astype(BF16)
    seg_off = _dot(exp_before, seg_rows_b)[:, 0:1]
    base = seg_off + rank_in_tile
    pos = [jnp.sum(jnp.where(sel, base, 0.0), axis=0, keepdims=True) for sel in sels]
    pos_ref[...] = jnp.concatenate(pos, axis=0).astype(jnp.int32)
    run = run_ref[...]
    seg_ref[0] = jnp.concatenate([seg_rows, seg_off, run], axis=1).astype(jnp.int32)
    run_ref[...] = run + seg_rows
    cnt_ref[...] = jnp.broadcast_to(run + seg_rows, cnt_ref.shape)


def _mixer_call(x, ua, sga, sgb, yattn, mod, poolw, pscale, wa, wb, wout, g, b_, rwt, rb):
    b, s, d = x.shape
    tm = min(TOKEN_TILE, s)
    n_i = s // tm
    hb = tm // POOL_HALO
    n_hb = s // POOL_HALO
    row = lambda bi, i: (bi, i, 0)
    c2 = lambda bi, i: (0, 0)
    c3 = lambda bi, i: (0, 0, 0)
    tokcol = lambda bi, i: (0, bi * n_i + i)
    n_tok = b * s
    return pl.pallas_call(
        functools.partial(_mixer_kernel, seq_len=s),
        grid=(b, n_i),
        in_specs=[pl.BlockSpec((1, tm, d), row),
                  pl.BlockSpec((1, tm, d), row),
                  pl.BlockSpec((1, POOL_HALO, d), lambda bi, i: (bi, jnp.maximum(i * hb - 1, 0), 0)),
                  pl.BlockSpec((1, POOL_HALO, d), lambda bi, i: (bi, jnp.minimum((i + 1) * hb, n_hb - 1), 0)),
                  pl.BlockSpec((1, tm, d), row),
                  pl.BlockSpec((1, tm, d), row),
                  pl.BlockSpec((1, tm, d), row),
                  pl.BlockSpec((1, 6, d), lambda bi, i: (bi, 0, 0)),
                  pl.BlockSpec(poolw.shape, c3),
                  pl.BlockSpec((1, d), c2),
                  pl.BlockSpec((d, d), c2),
                  pl.BlockSpec((d, d), c2),
                  pl.BlockSpec((d, d), c2),
                  pl.BlockSpec((1, d), c2),
                  pl.BlockSpec((1, d), c2),
                  pl.BlockSpec((N_EXPERTS, d), c2),
                  pl.BlockSpec((N_EXPERTS, 1), c2)],
        out_specs=[pl.BlockSpec((1, tm, d), row),
                   pl.BlockSpec((TOP_K, tm), tokcol),
                   pl.BlockSpec((TOP_K, tm), tokcol),
                   pl.BlockSpec((1, N_EXPERTS, 3), lambda bi, i: (bi * n_i + i, 0, 0)),
                   pl.BlockSpec((N_EXPERTS, 128), c2)],
        out_shape=[jax.ShapeDtypeStruct((b, s, d), F32),
                   jax.ShapeDtypeStruct((TOP_K, n_tok), jnp.int32),
                   jax.ShapeDtypeStruct((TOP_K, n_tok), F32),
                   jax.ShapeDtypeStruct((b * n_i, N_EXPERTS, 3), jnp.int32),
                   jax.ShapeDtypeStruct((N_EXPERTS, 128), F32)],
        scratch_shapes=[pltpu.VMEM((tm + 2 * POOL_HALO, d), F32),
                        pltpu.VMEM((N_EXPERTS, 1), F32)],
        compiler_params=_params(2),
        name="mixer",
    )(x, ua, ua, ua, sga, sgb, yattn, mod, poolw, pscale, wa, wb, wout, g, b_, rwt, rb)


SEG_ALIGN = 8
SORTED_ROWS = TOP_K * MOE_TILE + SEG_ALIGN * N_EXPERTS
SEG_BITS = tuple(range(MOE_TILE.bit_length() - 1, SEG_ALIGN.bit_length() - 2, -1))


def _segment_copies(seg_ref, make_copy, action):
    def body(e, c):
        cnt = seg_ref[0, e, 0]
        src = seg_ref[0, e, 1]
        dst = seg_ref[0, e, 2]
        for bit in SEG_BITS:
            size = 1 << bit
            on = (cnt & size) != 0

            @pl.when(on)
            def _(src=src, dst=dst, size=size):
                action(make_copy(pl.multiple_of(src, SEG_ALIGN), pl.multiple_of(dst, SEG_ALIGN), size))

            step = jnp.where(on, size, 0)
            src = src + step
            dst = dst + step
        return c

    lax.fori_loop(0, N_EXPERTS, body, 0)


def _sort_matrix(pos, n_sorted, transposed):
    if transposed:
        grid = lax.broadcasted_iota(jnp.int32, (pos.shape[0], n_sorted), 1)
        hit = grid == pos[:, 0:1]
        for k in range(1, TOP_K):
            hit = hit | (grid == pos[:, k:k + 1])
    else:
        grid = lax.broadcasted_iota(jnp.int32, (n_sorted, pos.shape[1]), 0)
        hit = grid == pos[0:1, :]
        for k in range(1, TOP_K):
            hit = hit | (grid == pos[k:k + 1, :])
    return jnp.where(hit, 1.0, 0.0).astype(BF16)


def _dispatch_kernel(seg_ref, x1_ref, pos_ref, mod_ref, xb_in_ref, xb_ref, sorted_ref, sem):
    del xb_in_ref
    td, d = x1_ref.shape
    sh2 = mod_ref[0, 3:4, :]
    sc2 = mod_ref[0, 4:5, :]
    u = (_ln_plain(x1_ref[...]) * (1.0 + sc2) + sh2).astype(BF16)
    rows = _dot(_sort_matrix(pos_ref[...], SORTED_ROWS, transposed=False), u)
    lo = lax.bitcast_convert_type(rows[:, :d // 2], jnp.uint32)
    hi = lax.bitcast_convert_type(rows[:, d // 2:], jnp.uint32)
    sorted_ref[...] = hi | (lo >> 16)

    def make_copy(src, dst, size):
        return pltpu.make_async_copy(sorted_ref.at[pl.ds(src, size)], xb_ref.at[pl.ds(dst, size)], sem)

    _segment_copies(seg_ref, make_copy, lambda cp: cp.start())
    _segment_copies(seg_ref, make_copy, lambda cp: cp.wait())


def _dispatch_call(seg, x1_flat, pos, mod, xb_zero, seq_len):
    n_tok, d = x1_flat.shape
    td = MOE_TILE
    return pl.pallas_call(
        _dispatch_kernel,
        grid=(n_tok // td,),
        in_specs=[pl.BlockSpec((1, N_EXPERTS, 3), lambda i: (i, 0, 0), memory_space=pltpu.SMEM),
                  pl.BlockSpec((td, d), lambda i: (i, 0)),
                  pl.BlockSpec((TOP_K, td), lambda i: (0, i)),
                  pl.BlockSpec((1, 6, d), lambda i: ((i * td) // seq_len, 0, 0)),
                  pl.BlockSpec(memory_space=pl.ANY)],
        out_specs=pl.BlockSpec(memory_space=pl.ANY),
        out_shape=jax.ShapeDtypeStruct(xb_zero.shape, jnp.uint32),
        scratch_shapes=[pltpu.VMEM((SORTED_ROWS, d // 2), jnp.uint32), pltpu.SemaphoreType.DMA],
        input_output_aliases={4: 0},
        compiler_params=_params(1, has_side_effects=True),
        name="dispatch",
    )(seg, x1_flat, pos, mod, xb_zero)


def _expert_kernel(blk_e_ref, n_used_ref, xb_ref, wup_ref, bup_ref, wdn_ref, bdn_ref, yb_ref):
    del blk_e_ref
    used = pl.program_id(0) < n_used_ref[0]

    @pl.when(jnp.logical_not(used))
    def _():
        yb_ref[...] = jnp.zeros_like(yb_ref)

    @pl.when(used)
    def _():
        words = xb_ref[...]
        lo = lax.bitcast_convert_type(words << 16, F32).astype(BF16)
        hi = lax.bitcast_convert_type(words & jnp.uint32(0xFFFF0000), F32).astype(BF16)
        half = D_MODEL // 2
        h = _dot(lo, wup_ref[0, :half, :]) + _dot(hi, wup_ref[0, half:, :]) + bup_ref[0]
        gate = jnp.minimum(h[:, :D_FF], SWIGLU_LIMIT)
        up = jnp.clip(h[:, D_FF:], -SWIGLU_LIMIT, SWIGLU_LIMIT)
        act = (up + 1.0) * (gate * jax.nn.sigmoid(SWIGLU_ALPHA * gate))
        yb_ref[...] = _dot(act.astype(BF16), wdn_ref[0]) + bdn_ref[0]


def _expert_call(blk_e, n_used, xb, wup, bup, wdn, bdn):
    n_rows = xb.shape[0]
    d = D_MODEL
    rows = EXPERT_ROWS
    last = lambda r, be, nu: jnp.maximum(jnp.minimum(r, nu[0] - 1), 0)
    exp3 = lambda r, be, nu: (be[last(r, be, nu)], 0, 0)
    return pl.pallas_call(
        _expert_kernel,
        grid_spec=pltpu.PrefetchScalarGridSpec(
            num_scalar_prefetch=2,
            grid=(n_rows // rows,),
            in_specs=[pl.BlockSpec((rows, d // 2), lambda r, be, nu: (last(r, be, nu), 0)),
                      pl.BlockSpec((1, d, 2 * D_FF), exp3),
                      pl.BlockSpec((1, 1, 2 * D_FF), exp3),
                      pl.BlockSpec((1, D_FF, d), exp3),
                      pl.BlockSpec((1, 1, d), exp3)],
            out_specs=pl.BlockSpec((rows, d), lambda r, be, nu: (r, 0))),
        out_shape=jax.ShapeDtypeStruct((n_rows, d), F32),
        compiler_params=_params(1),
        name="experts",
    )(blk_e, n_used, xb, wup, bup, wdn, bdn)


def _combine_kernel(seg_ref, x1_ref, pos_ref, post_ref, gates_ref, mod_ref, g_ref, b_ref, yb_ref,
                    o_ref, ybuf_ref, sem):
    tc = x1_ref.shape[0]
    n_sorted = SORTED_ROWS
    ybuf_ref[TOP_K * tc:, :] = jnp.zeros((n_sorted - TOP_K * tc, ybuf_ref.shape[1]), F32)

    def make_copy(src, dst, size):
        return pltpu.make_async_copy(yb_ref.at[pl.ds(dst, size)], ybuf_ref.at[pl.ds(src, size)], sem)

    _segment_copies(seg_ref, make_copy, lambda cp: cp.start())

    pos = pos_ref[...]
    gates = gates_ref[...]
    grid = lax.broadcasted_iota(jnp.int32, (n_sorted, tc), 0)
    gsel = jnp.where(grid == pos[0:1, :], gates[0:1, :], 0.0)
    for k in range(1, TOP_K):
        gsel = gsel + jnp.where(grid == pos[k:k + 1, :], gates[k:k + 1, :], 0.0)
    gate_col = jnp.sum(gsel, axis=1, keepdims=True)
    unsort = _sort_matrix(post_ref[...], n_sorted, transposed=True)

    _segment_copies(seg_ref, make_copy, lambda cp: cp.wait())

    yg = ybuf_ref[...] * gate_col
    y_hi = yg.astype(BF16)
    y_lo = (yg - y_hi.astype(F32)).astype(BF16)
    h2 = _dot(unsort, y_hi) + _dot(unsort, y_lo)
    g2 = mod_ref[0, 5:6, :]
    o_ref[...] = _ln_plain(DEEPNORM_ALPHA * x1_ref[...] + g2 * h2) * g_ref[...] + b_ref[...]


def _combine_call(seg, x1_flat, pos, pos_t, gates, mod, g, b_, yb, seq_len):
    n_tok, d = x1_flat.shape
    tc = MOE_TILE
    c2 = lambda i: (0, 0)
    return pl.pallas_call(
        _combine_kernel,
        grid=(n_tok // tc,),
        in_specs=[pl.BlockSpec((1, N_EXPERTS, 3), lambda i: (i, 0, 0), memory_space=pltpu.SMEM),
                  pl.BlockSpec((tc, d), lambda i: (i, 0)),
                  pl.BlockSpec((TOP_K, tc), lambda i: (0, i)),
                  pl.BlockSpec((tc, TOP_K), lambda i: (i, 0)),
                  pl.BlockSpec((TOP_K, tc), lambda i: (0, i)),
                  pl.BlockSpec((1, 6, d), lambda i: ((i * tc) // seq_len, 0, 0)),
                  pl.BlockSpec((1, d), c2),
                  pl.BlockSpec((1, d), c2),
                  pl.BlockSpec(memory_space=pl.ANY)],
        out_specs=pl.BlockSpec((tc, d), lambda i: (i, 0)),
        out_shape=jax.ShapeDtypeStruct((n_tok, d), F32),
        scratch_shapes=[pltpu.VMEM((SORTED_ROWS, d), F32), pltpu.SemaphoreType.DMA],
        compiler_params=_params(1),
        name="combine",
    )(seg, x1_flat, pos, pos_t, gates, mod, g, b_, yb)


def _rope_tables(seq_len):
    inv_freq = ROPE_THETA ** (-jnp.arange(0, HEAD_DIM, 2, dtype=F32) / HEAD_DIM)
    ang = jnp.arange(seq_len, dtype=F32)[:, None] * inv_freq[None, :]
    cos, sin = lax.optimization_barrier((jnp.cos(ang), jnp.sin(ang)))
    zero = jnp.zeros_like(sin)
    reps = HEAD_W // HEAD_DIM
    cos_t = jnp.tile(jnp.concatenate([cos, cos], axis=1), (1, reps))
    sina_t = jnp.tile(jnp.concatenate([-sin, zero], axis=1), (1, reps))
    sinb_t = jnp.tile(jnp.concatenate([zero, sin], axis=1), (1, reps))
    return cos_t, sina_t, sinb_t


def _encode(x, mod, w):
    b, s, d = x.shape
    n_tok = b * s
    cos, sina, sinb = w["rope"]
    ua, q, k, vt, sga, sgb = _inproj_call(x, mod, cos, sina, sinb, w["wcat"], w["wvt"])
    yattn = _attn_call(q, k, vt, w["lq1"], w["lk1"], w["lq2"], w["lk2"], w["subw"])
    x1, pos, gates, seg, cnt = _mixer_call(
        x, ua, sga, sgb, yattn, mod, w["poolw"], w["pscale"], w["wa"], w["wb"], w["wout"],
        w["ln1_g"], w["ln1_b"], w["rwt"], w["rb"])

    rows = EXPERT_ROWS
    counts = cnt[:, 0].astype(jnp.int32)
    padded = (counts + rows - 1) // rows * rows
    pends = jnp.cumsum(padded)
    pstarts = pends - padded
    seg = seg.at[:, :, 2].add(pstarts[None, :])
    n_tiles = n_tok // MOE_TILE
    n_blocks = (n_tiles * SORTED_ROWS) // rows + N_EXPERTS
    blk_row0 = jnp.arange(n_blocks, dtype=jnp.int32) * rows
    blk_e = jnp.minimum(jnp.sum((pends[None, :] <= blk_row0[:, None]).astype(jnp.int32), axis=1),
                        N_EXPERTS - 1)
    n_used = (pends[-1:] // rows).astype(jnp.int32)

    x1_flat = x1.reshape(n_tok, d)
    xb = _dispatch_call(seg, x1_flat, pos, mod, jnp.zeros((n_blocks * rows, d // 2), jnp.uint32), s)
    yb = _expert_call(blk_e, n_used, xb, w["wup"], w["bup"], w["wdn"], w["bdn"])
    out = _combine_call(seg, x1_flat, pos, pos.T, gates, mod, w["ln2_g"], w["ln2_b"], yb, s)
    return out.reshape(b, s, d)


def kernel(x_prompt, x_sample, c_prompt, c_sample, w_ada, b_ada, w_in, pool_w, pool_scale, w_branch_a, w_branch_b, lambda_q1, lambda_k1, lambda_q2, lambda_k2, subln_w, w_out, ln1_g, ln1_b, router_w, router_b, exp_w_up, exp_b_up, exp_w_down, exp_b_down, ln2_g, ln2_b):
    d = D_MODEL
    w_in0 = w_in[0]
    seg = lambda j: w_in0[:, j * d:(j + 1) * d]
    w = {
        "wcat": jnp.concatenate([seg(0), seg(1), seg(2), seg(4), seg(5)], axis=1).astype(BF16),
        "wvt": seg(3).T.astype(BF16),
        "lq1": lambda_q1, "lk1": lambda_k1, "lq2": lambda_q2, "lk2": lambda_k2,
        "subw": subln_w[0].reshape(HEAD_W, 1),
        "poolw": pool_w[0].astype(BF16),
        "pscale": pool_scale,
        "wa": w_branch_a[0].astype(BF16),
        "wb": w_branch_b[0].astype(BF16),
        "wout": w_out[0].astype(BF16),
        "ln1_g": ln1_g, "ln1_b": ln1_b,
        "rwt": router_w[0].T.astype(BF16),
        "rb": router_b[0].reshape(N_EXPERTS, 1),
        "wup": exp_w_up[0].astype(BF16),
        "bup": exp_b_up[0].reshape(N_EXPERTS, 1, 2 * D_FF),
        "wdn": exp_w_down[0].astype(BF16),
        "bdn": exp_b_down[0].reshape(N_EXPERTS, 1, d),
        "ln2_g": ln2_g, "ln2_b": ln2_b,
        "rope": _rope_tables(max(x_prompt.shape[1], x_sample.shape[1])),
    }
    nb_p, nb_s = c_prompt.shape[0], c_sample.shape[0]
    pad = (-(nb_p + nb_s)) % 8
    c_all = jnp.concatenate([c_prompt, c_sample, jnp.zeros((pad, d), F32)], axis=0)
    mod = _mod_call(c_all, w_ada[0], b_ada[0]).reshape(-1, 6, d)
    y_prompt = _encode(x_prompt, mod[:nb_p], w)
    y_sample = _encode(x_sample, mod[nb_p:nb_p + nb_s], w)
    return (y_prompt, y_sample)
```

```python
import functools

import jax
import jax.numpy as jnp
from jax import lax
from jax.experimental import pallas as pl
from jax.experimental.pallas import tpu as pltpu

F32 = jnp.float32
BF16 = jnp.bfloat16

D_MODEL = 1024
N_HEADS = 8
HEAD_DIM = 64
HEAD_W = 2 * HEAD_DIM
ROPE_THETA = 10000.0
POOL_WINDOWS = (2, 4, 8, 16)
POOL_GC = 256
POOL_HALO = 8
N_EXPERTS = 32
TOP_K = 4
D_FF = 1024
SWIGLU_LIMIT = 7.0
SWIGLU_ALPHA = 1.702
LN_EPS = 1e-5
DEPTH = 1
DEEPNORM_ALPHA = (2.0 * DEPTH) ** 0.25
LAMBDA_INIT = 0.8 - 0.6 * 1.0

VMEM_LIMIT_V7X = 56 * 1024 * 1024

TOKEN_TILE = 256
MOE_TILE = TOKEN_TILE
ATTN_Q_TILE = 1024
ATTN_K_CHUNK = 512
ONES_ROWS = 16
LOG2_E = 1.4426950408889634
EXPERT_ROWS = 256


def _params(n_axes, **kw):
    return pltpu.CompilerParams(dimension_semantics=("arbitrary",) * n_axes,
                                vmem_limit_bytes=VMEM_LIMIT_V7X, **kw)


def _ln_plain(x):
    mu = jnp.mean(x, axis=-1, keepdims=True)
    xc = x - mu
    var = jnp.mean(xc * xc, axis=-1, keepdims=True)
    return xc * lax.rsqrt(var + LN_EPS)


def _dot(a, b):
    return jnp.dot(a, b, preferred_element_type=F32)


def _dot_nt(a, b, **kw):
    return lax.dot_general(a, b, (((1,), (1,)), ((), ())), preferred_element_type=F32, **kw)


def _mod_kernel(c_ref, w_ref, b_ref, o_ref):
    c = c_ref[...]
    s = c * jax.nn.sigmoid(c)
    o_ref[...] = jnp.dot(s, w_ref[...], preferred_element_type=F32,
                         precision=lax.Precision.HIGHEST) + b_ref[...]


def _mod_call(c, w_ada, b_ada):
    rows, d = c.shape
    n_out = w_ada.shape[1]
    return pl.pallas_call(
        _mod_kernel,
        grid=(n_out // d,),
        in_specs=[pl.BlockSpec((rows, d), lambda j: (0, 0)),
                  pl.BlockSpec((d, d), lambda j: (0, j)),
                  pl.BlockSpec((1, d), lambda j: (0, j))],
        out_specs=pl.BlockSpec((rows, d), lambda j: (0, j)),
        out_shape=jax.ShapeDtypeStruct((rows, n_out), F32),
        compiler_params=_params(1),
        name="mod",
    )(c, w_ada, b_ada.reshape(1, n_out))


def _inproj_kernel(x_ref, mod_ref, cos_ref, sina_ref, sinb_ref, wcat_ref, wvt_ref,
                   ua_ref, q_ref, k_ref, vt_ref, sga_ref, sgb_ref):
    d = D_MODEL
    x = x_ref[0]
    sh1 = mod_ref[0, 0:1, :]
    sc1 = mod_ref[0, 1:2, :]
    u = (_ln_plain(x) * (1.0 + sc1) + sh1).astype(BF16)

    def proj(j):
        return _dot(u, wcat_ref[:, j * d:(j + 1) * d])

    reps = d // HEAD_W
    cos = jnp.concatenate([cos_ref[...]] * reps, axis=1)
    sina = jnp.concatenate([sina_ref[...]] * reps, axis=1)
    sinb = jnp.concatenate([sinb_ref[...]] * reps, axis=1)

    def rope(t):
        half = HEAD_DIM // 2
        return t * cos + pltpu.roll(t, d - half, 1) * sina + pltpu.roll(t, half, 1) * sinb

    ua_ref[0] = proj(0)
    q_ref[0] = (rope(proj(1)) * (HEAD_DIM ** -0.5 * LOG2_E)).astype(BF16)
    k_ref[0] = rope(proj(2)).astype(BF16)
    vt_ref[0] = _dot_nt(wvt_ref[...], u).astype(BF16)
    sga_ref[0] = jax.nn.sigmoid(proj(3))
    sgb_ref[0] = jax.nn.sigmoid(proj(4))


def _inproj_call(x, mod, cos, sina, sinb, wcat, wvt):
    b, s, d = x.shape
    tm = min(TOKEN_TILE, s)
    row = lambda bi, i: (bi, i, 0)
    const2 = lambda bi, i: (0, 0)
    tab = pl.BlockSpec((tm, HEAD_W), lambda bi, i: (i, 0))
    return pl.pallas_call(
        _inproj_kernel,
        grid=(b, s // tm),
        in_specs=[pl.BlockSpec((1, tm, d), row),
                  pl.BlockSpec((1, 6, d), lambda bi, i: (bi, 0, 0)),
                  tab, tab, tab,
                  pl.BlockSpec(wcat.shape, const2),
                  pl.BlockSpec(wvt.shape, const2)],
        out_specs=[pl.BlockSpec((1, tm, d), row),
                   pl.BlockSpec((1, tm, d), row),
                   pl.BlockSpec((1, tm, d), row),
                   pl.BlockSpec((1, d, tm), lambda bi, i: (bi, 0, i)),
                   pl.BlockSpec((1, tm, d), row),
                   pl.BlockSpec((1, tm, d), row)],
        out_shape=[jax.ShapeDtypeStruct((b, s, d), F32),
                   jax.ShapeDtypeStruct((b, s, d), BF16),
                   jax.ShapeDtypeStruct((b, s, d), BF16),
                   jax.ShapeDtypeStruct((b, d, s), BF16),
                   jax.ShapeDtypeStruct((b, s, d), F32),
                   jax.ShapeDtypeStruct((b, s, d), F32)],
        compiler_params=_params(2),
        name="inproj",
    )(x, mod, cos, sina, sinb, wcat, wvt)


def _attn_kernel(q_ref, k_ref, vt_ref, lq1_ref, lk1_ref, lq2_ref, lk2_ref, subw_ref,
                 o_ref, qt_ref, sa_ref, sb_ref, o1_ref, o2_ref, *, k_chunk):
    tq = q_ref.shape[1]
    s_len = k_ref.shape[1]
    q = q_ref[0]
    lane = lax.broadcasted_iota(jnp.int32, q.shape, 1)
    zero = jnp.zeros_like(q)
    qt_ref[0] = jnp.where(lane < HEAD_DIM, q, zero).T
    qt_ref[1] = jnp.where(lane >= HEAD_DIM, q, zero).T
    o1_ref[...] = jnp.zeros_like(o1_ref)
    o2_ref[...] = jnp.zeros_like(o2_ref)
    n_chunks = s_len // k_chunk
    ones = jnp.ones((ONES_ROWS, k_chunk), BF16)

    def scores(c, s_ref):
        start = pl.multiple_of(c * k_chunk, k_chunk)
        kc = k_ref[0, pl.ds(start, k_chunk), :]
        s1 = _dot(kc, qt_ref[0])
        s2 = _dot(kc, qt_ref[1])
        s_ref[0] = s1
        s_ref[1] = s2
        return jnp.max(s1, axis=0, keepdims=True), jnp.max(s2, axis=0, keepdims=True)

    def accumulate(c, s_ref, cmax, m):
        start = pl.multiple_of(c * k_chunk, k_chunk)
        vtc = jnp.concatenate([vt_ref[0, :, pl.ds(start, k_chunk)], ones], axis=0)
        m_out = []
        for j, acc_ref in enumerate((o1_ref, o2_ref)):
            m_new = jnp.maximum(m[j], cmax[j])
            alpha = jnp.exp2(m[j] - m_new)
            p = jnp.exp2(s_ref[j] - m_new).astype(BF16)
            acc_ref[...] = alpha * acc_ref[...] + _dot(vtc, p)
            m_out.append(m_new)
        return tuple(m_out)

    def body(i, carry):
        cmax_a, m = carry
        cmax_b = scores(2 * i + 1, sb_ref)
        m = accumulate(2 * i, sa_ref, cmax_a, m)
        cmax_a = scores(2 * i + 2, sa_ref)
        m = accumulate(2 * i + 1, sb_ref, cmax_b, m)
        return cmax_a, m

    neg = jnp.full((1, tq), -jnp.inf, F32)
    cmax_a, m = lax.fori_loop(0, n_chunks // 2 - 1, body, (scores(0, sa_ref), (neg, neg)))
    cmax_b = scores(n_chunks - 1, sb_ref)
    m = accumulate(n_chunks - 2, sa_ref, cmax_a, m)
    accumulate(n_chunks - 1, sb_ref, cmax_b, m)

    lam = (jnp.exp(jnp.sum(lq1_ref[...] * lk1_ref[...], axis=1, keepdims=True))
           - jnp.exp(jnp.sum(lq2_ref[...] * lk2_ref[...], axis=1, keepdims=True))
           + LAMBDA_INIT)
    acc1 = o1_ref[...]
    acc2 = o2_ref[...]
    o = (acc1[:HEAD_W] / acc1[HEAD_W:HEAD_W + 1]
         - lam * (acc2[:HEAD_W] / acc2[HEAD_W:HEAD_W + 1]))
    o = o * lax.rsqrt(jnp.mean(o * o, axis=0, keepdims=True) + LN_EPS)
    o = o * subw_ref[...] * (1.0 - LAMBDA_INIT)
    o_ref[0] = o.T.astype(o_ref.dtype)


def _attn_call(q, k, vt, lq1, lk1, lq2, lk2, subw):
    b, s, d = q.shape
    tq = min(ATTN_Q_TILE, s)
    kc = min(ATTN_K_CHUNK, s // 2)
    assert s % (2 * kc) == 0 and s % tq == 0
    lam_spec = pl.BlockSpec((1, HEAD_DIM), lambda bi, h, i: (0, 0))
    return pl.pallas_call(
        functools.partial(_attn_kernel, k_chunk=kc),
        grid=(b, N_HEADS, s // tq),
        in_specs=[pl.BlockSpec((1, tq, HEAD_W), lambda bi, h, i: (bi, i, h)),
                  pl.BlockSpec((1, s, HEAD_W), lambda bi, h, i: (bi, 0, h)),
                  pl.BlockSpec((1, HEAD_W, s), lambda bi, h, i: (bi, h, 0)),
                  lam_spec, lam_spec, lam_spec, lam_spec,
                  pl.BlockSpec((HEAD_W, 1), lambda bi, h, i: (0, 0))],
        out_specs=pl.BlockSpec((1, tq, HEAD_W), lambda bi, h, i: (bi, i, h)),
        out_shape=jax.ShapeDtypeStruct((b, s, d), BF16),
        scratch_shapes=[pltpu.VMEM((2, HEAD_W, tq), BF16),
                        pltpu.VMEM((2, kc, tq), F32), pltpu.VMEM((2, kc, tq), F32),
                        pltpu.VMEM((HEAD_W + ONES_ROWS, tq), F32),
                        pltpu.VMEM((HEAD_W + ONES_ROWS, tq), F32)],
        compiler_params=_params(3),
        name="attn",
    )(q, k, vt, lq1, lk1, lq2, lk2, subw)


def _mixer_kernel(x_ref, ua_ref, uprev_ref, unext_ref, sga_ref, sgb_ref, yattn_ref, mod_ref,
                  poolw_ref, pscale_ref, wa_ref, wb_ref, wout_ref, g_ref, b_ref, rwt_ref, rb_ref,
                  x1_ref, pos_ref, gate_ref, seg_ref, cnt_ref,
                  ext_ref, run_ref, *, seq_len):
    tm = x_ref.shape[1]
    halo = POOL_HALO
    bi = pl.program_id(0)
    i = pl.program_id(1)
    n_i = pl.num_programs(1)

    @pl.when((bi == 0) & (i == 0))
    def _():
        run_ref[...] = jnp.zeros_like(run_ref)

    ua = ua_ref[0]
    ext_ref[0:halo, :] = jnp.where(i > 0, uprev_ref[0], 0.0)
    ext_ref[halo:halo + tm, :] = ua
    ext_ref[halo + tm:, :] = jnp.where(i < n_i - 1, unext_ref[0], 0.0)
    pos = i * tm + lax.broadcasted_iota(jnp.int32, (tm, 1), 0)
    mixed = []
    for g, w in enumerate(POOL_WINDOWS):
        cols = slice(g * POOL_GC, (g + 1) * POOL_GC)
        win = ext_ref[halo - w // 2:halo - w // 2 + tm, cols]
        for j in range(1, w):
            off = halo - w // 2 + j
            win = win + ext_ref[off:off + tm, cols]
        lo = jnp.maximum(pos - w // 2, 0)
        hi = jnp.minimum(pos + w // 2 - 1, seq_len - 1)
        cnt = (hi - lo + 1).astype(F32)
        pooled = win / cnt - ua[:, cols]
        mixed.append(_dot(pooled.astype(BF16), poolw_ref[g]))
    mixed = jnp.concatenate(mixed, axis=1) * pscale_ref[...]
    ya = _dot(mixed.astype(BF16), wa_ref[...])

    yb = _dot(yattn_ref[0], wb_ref[...])
    merged = sga_ref[0] * ya + sgb_ref[0] * yb
    h = _dot(merged.astype(BF16), wout_ref[...])
    g1 = mod_ref[0, 2:3, :]
    x1 = _ln_plain(DEEPNORM_ALPHA * x_ref[0] + g1 * h) * g_ref[...] + b_ref[...]
    x1_ref[0] = x1

    sh2 = mod_ref[0, 3:4, :]
    sc2 = mod_ref[0, 4:5, :]
    u2 = _ln_plain(x1) * (1.0 + sc2) + sh2
    logits = _dot_nt(rwt_ref[...], u2.astype(BF16)) + rb_ref[...]
    eio = lax.broadcasted_iota(jnp.int32, logits.shape, 0)
    vals, sels = [], []
    l = logits
    for _ in range(TOP_K):
        mv = jnp.max(l, axis=0, keepdims=True)
        idx = jnp.min(jnp.where(l == mv, eio, N_EXPERTS), axis=0, keepdims=True)
        sel = eio == idx
        vals.append(mv)
        sels.append(sel)
        l = jnp.where(sel, -jnp.inf, l)
    ex = [jnp.exp(v - vals[0]) for v in vals]
    den = ex[0] + ex[1] + ex[2] + ex[3]
    gate_ref[...] = jnp.concatenate([e / den for e in ex], axis=0)

    member = jnp.zeros(logits.shape, F32)
    for sel in sels:
        member = member + sel.astype(F32)
    member_b = member.astype(BF16)
    tok_before = (lax.broadcasted_iota(jnp.int32, (tm, tm), 0)
                  < lax.broadcasted_iota(jnp.int32, (tm, tm), 1)).astype(BF16)
    exp_before = (lax.broadcasted_iota(jnp.int32, (N_EXPERTS, N_EXPERTS), 1)
                  < lax.broadcasted_iota(jnp.int32, (N_EXPERTS, N_EXPERTS), 0)).astype(BF16)
    rank_in_tile = _dot(member_b, tok_before)
    seg_cnt = jnp.sum(member, axis=1, keepdims=True)
    seg_rows = jnp.ceil(seg_cnt * (1.0 / SEG_ALIGN)) * SEG_ALIGN
    seg_rows_b = jnp.broadcast_to(seg_rows, (N_EXPERTS, 128)).astype(BF16)
    seg_off = _dot(exp_before, seg_rows_b)[:, 0:1]
    base = seg_off + rank_in_tile
    pos = [jnp.sum(jnp.where(sel, base, 0.0), axis=0, keepdims=True) for sel in sels]
    pos_ref[...] = jnp.concatenate(pos, axis=0).astype(jnp.int32)
    run = run_ref[...]
    seg_ref[0] = jnp.concatenate([seg_rows, seg_off, run], axis=1).astype(jnp.int32)
    run_ref[...] = run + seg_rows
    cnt_ref[...] = jnp.broadcast_to(run + seg_rows, cnt_ref.shape)


def _mixer_call(x, ua, sga, sgb, yattn, mod, poolw, pscale, wa, wb, wout, g, b_, rwt, rb):
    b, s, d = x.shape
    tm = min(TOKEN_TILE, s)
    n_i = s // tm
    hb = tm // POOL_HALO
    n_hb = s // POOL_HALO
    row = lambda bi, i: (bi, i, 0)
    c2 = lambda bi, i: (0, 0)
    c3 = lambda bi, i: (0, 0, 0)
    tokcol = lambda bi, i: (0, bi * n_i + i)
    n_tok = b * s
    return pl.pallas_call(
        functools.partial(_mixer_kernel, seq_len=s),
        grid=(b, n_i),
        in_specs=[pl.BlockSpec((1, tm, d), row),
                  pl.BlockSpec((1, tm, d), row),
                  pl.BlockSpec((1, POOL_HALO, d), lambda bi, i: (bi, jnp.maximum(i * hb - 1, 0), 0)),
                  pl.BlockSpec((1, POOL_HALO, d), lambda bi, i: (bi, jnp.minimum((i + 1) * hb, n_hb - 1), 0)),
                  pl.BlockSpec((1, tm, d), row),
                  pl.BlockSpec((1, tm, d), row),
                  pl.BlockSpec((1, tm, d), row),
                  pl.BlockSpec((1, 6, d), lambda bi, i: (bi, 0, 0)),
                  pl.BlockSpec(poolw.shape, c3),
                  pl.BlockSpec((1, d), c2),
                  pl.BlockSpec((d, d), c2),
                  pl.BlockSpec((d, d), c2),
                  pl.BlockSpec((d, d), c2),
                  pl.BlockSpec((1, d), c2),
                  pl.BlockSpec((1, d), c2),
                  pl.BlockSpec((N_EXPERTS, d), c2),
                  pl.BlockSpec((N_EXPERTS, 1), c2)],
        out_specs=[pl.BlockSpec((1, tm, d), row),
                   pl.BlockSpec((TOP_K, tm), tokcol),
                   pl.BlockSpec((TOP_K, tm), tokcol),
                   pl.BlockSpec((1, N_EXPERTS, 3), lambda bi, i: (bi * n_i + i, 0, 0)),
                   pl.BlockSpec((N_EXPERTS, 128), c2)],
        out_shape=[jax.ShapeDtypeStruct((b, s, d), F32),
                   jax.ShapeDtypeStruct((TOP_K, n_tok), jnp.int32),
                   jax.ShapeDtypeStruct((TOP_K, n_tok), F32),
                   jax.ShapeDtypeStruct((b * n_i, N_EXPERTS, 3), jnp.int32),
                   jax.ShapeDtypeStruct((N_EXPERTS, 128), F32)],
        scratch_shapes=[pltpu.VMEM((tm + 2 * POOL_HALO, d), F32),
                        pltpu.VMEM((N_EXPERTS, 1), F32)],
        compiler_params=_params(2),
        name="mixer",
    )(x, ua, ua, ua, sga, sgb, yattn, mod, poolw, pscale, wa, wb, wout, g, b_, rwt, rb)


SEG_ALIGN = 8
SORTED_ROWS = TOP_K * MOE_TILE + SEG_ALIGN * N_EXPERTS
SEG_BITS = tuple(range(MOE_TILE.bit_length() - 1, SEG_ALIGN.bit_length() - 2, -1))


def _segment_copies(seg_ref, make_copy, action):
    def body(e, c):
        cnt = seg_ref[0, e, 0]
        src = seg_ref[0, e, 1]
        dst = seg_ref[0, e, 2]
        for bit in SEG_BITS:
            size = 1 << bit
            on = (cnt & size) != 0

            @pl.when(on)
            def _(src=src, dst=dst, size=size):
                action(make_copy(pl.multiple_of(src, SEG_ALIGN), pl.multiple_of(dst, SEG_ALIGN), size))

            step = jnp.where(on, size, 0)
            src = src + step
            dst = dst + step
        return c

    lax.fori_loop(0, N_EXPERTS, body, 0)


def _sort_matrix(pos, n_sorted, transposed):
    if transposed:
        grid = lax.broadcasted_iota(jnp.int32, (pos.shape[0], n_sorted), 1)
        hit = grid == pos[:, 0:1]
        for k in range(1, TOP_K):
            hit = hit | (grid == pos[:, k:k + 1])
    else:
        grid = lax.broadcasted_iota(jnp.int32, (n_sorted, pos.shape[1]), 0)
        hit = grid == pos[0:1, :]
        for k in range(1, TOP_K):
            hit = hit | (grid == pos[k:k + 1, :])
    return jnp.where(hit, 1.0, 0.0).astype(BF16)


def _dispatch_kernel(pends_ref, padded_ref, n_used_ref,
                     seg_ref, x1_ref, pos_ref, mod_ref, xb_ref, sorted_ref, zero_ref, sem):
    td, d = x1_ref.shape
    n_blocks = xb_ref.shape[0] // EXPERT_ROWS

    @pl.when(pl.program_id(0) == 0)
    def _():
        zero_ref[...] = jnp.zeros_like(zero_ref)

        def zero_copy(row0):
            return pltpu.make_async_copy(
                zero_ref, xb_ref.at[pl.ds(pl.multiple_of(row0, EXPERT_ROWS), EXPERT_ROWS)], sem)

        def zero_blocks(action):
            def per_expert(e, c):
                @pl.when(padded_ref[e] > 0)
                def _():
                    action(zero_copy(pends_ref[e] - EXPERT_ROWS))
                return c

            def per_unused(r, c):
                action(zero_copy(r * EXPERT_ROWS))
                return c

            lax.fori_loop(0, N_EXPERTS, per_expert, 0)
            lax.fori_loop(n_used_ref[0], n_blocks, per_unused, 0)

        zero_blocks(lambda cp: cp.start())
        zero_blocks(lambda cp: cp.wait())

    sh2 = mod_ref[0, 3:4, :]
    sc2 = mod_ref[0, 4:5, :]
    u = (_ln_plain(x1_ref[...]) * (1.0 + sc2) + sh2).astype(BF16)
    rows = _dot(_sort_matrix(pos_ref[...], SORTED_ROWS, transposed=False), u)
    lo = lax.bitcast_convert_type(rows[:, :d // 2], jnp.uint32)
    hi = lax.bitcast_convert_type(rows[:, d // 2:], jnp.uint32)
    sorted_ref[...] = hi | (lo >> 16)

    def make_copy(src, dst, size):
        return pltpu.make_async_copy(sorted_ref.at[pl.ds(src, size)], xb_ref.at[pl.ds(dst, size)], sem)

    _segment_copies(seg_ref, make_copy, lambda cp: cp.start())
    _segment_copies(seg_ref, make_copy, lambda cp: cp.wait())


def _dispatch_call(pends, padded, n_used, seg, x1_flat, pos, mod, n_rows, seq_len):
    n_tok, d = x1_flat.shape
    td = MOE_TILE
    return pl.pallas_call(
        _dispatch_kernel,
        grid_spec=pltpu.PrefetchScalarGridSpec(
            num_scalar_prefetch=3,
            grid=(n_tok // td,),
            in_specs=[pl.BlockSpec((1, N_EXPERTS, 3), lambda i, *_: (i, 0, 0), memory_space=pltpu.SMEM),
                      pl.BlockSpec((td, d), lambda i, *_: (i, 0)),
                      pl.BlockSpec((TOP_K, td), lambda i, *_: (0, i)),
                      pl.BlockSpec((1, 6, d), lambda i, *_: ((i * td) // seq_len, 0, 0))],
            out_specs=pl.BlockSpec(memory_space=pl.ANY),
            scratch_shapes=[pltpu.VMEM((SORTED_ROWS, d // 2), jnp.uint32),
                            pltpu.VMEM((EXPERT_ROWS, d // 2), jnp.uint32),
                            pltpu.SemaphoreType.DMA]),
        out_shape=jax.ShapeDtypeStruct((n_rows, d // 2), jnp.uint32),
        compiler_params=_params(1, has_side_effects=True),
        name="dispatch",
    )(pends, padded, n_used, seg, x1_flat, pos, mod)


CAST_ROWS = 128


def _expert_kernel(blk_e_ref, n_used_ref, first_ref, slot_ref, next_ref,
                   xb_ref, bup_ref, bdn_ref, wup_hbm, wdn_hbm, yb_ref,
                   wup32_ref, wdn32_ref, wup_ref, wdn_ref, sems):
    r = pl.program_id(0)
    used = r < n_used_ref[0]

    def weight_copies(e, slot):
        return (pltpu.make_async_copy(wup_hbm.at[e], wup32_ref.at[slot], sems.at[slot, 0]),
                pltpu.make_async_copy(wdn_hbm.at[e], wdn32_ref.at[slot], sems.at[slot, 1]))

    @pl.when(jnp.logical_not(used))
    def _():
        yb_ref[...] = jnp.zeros_like(yb_ref)

    @pl.when(used & (first_ref[r] == 1))
    def _():
        slot = slot_ref[r]

        @pl.when(r == 0)
        def _():
            for cp in weight_copies(blk_e_ref[r], slot):
                cp.start()

        for cp in weight_copies(blk_e_ref[r], slot):
            cp.wait()
        nxt = next_ref[r]

        @pl.when(nxt >= 0)
        def _():
            for cp in weight_copies(nxt, 1 - slot):
                cp.start()

        def cast(i, c):
            rows_ = pl.ds(pl.multiple_of(i * CAST_ROWS, CAST_ROWS), CAST_ROWS)
            wup_ref[rows_, :] = wup32_ref[slot, rows_, :].astype(BF16)
            wdn_ref[rows_, :] = wdn32_ref[slot, rows_, :].astype(BF16)
            return c

        lax.fori_loop(0, D_MODEL // CAST_ROWS, cast, 0)

    @pl.when(used)
    def _():
        words = xb_ref[...]
        lo = lax.bitcast_convert_type(words << 16, F32).astype(BF16)
        hi = lax.bitcast_convert_type(words & jnp.uint32(0xFFFF0000), F32).astype(BF16)
        half = D_MODEL // 2
        h = _dot(lo, wup_ref[:half, :]) + _dot(hi, wup_ref[half:, :]) + bup_ref[0]
        gate = jnp.minimum(h[:, :D_FF], SWIGLU_LIMIT)
        up = jnp.clip(h[:, D_FF:], -SWIGLU_LIMIT, SWIGLU_LIMIT)
        act = (up + 1.0) * (gate * jax.nn.sigmoid(SWIGLU_ALPHA * gate))
        yb_ref[...] = _dot(act.astype(BF16), wdn_ref[...]) + bdn_ref[0]


def _expert_call(blk_e, n_used, blk_first, blk_slot, blk_next, xb, wup, bup, wdn, bdn):
    n_rows = xb.shape[0]
    d = D_MODEL
    assert D_FF == d
    rows = EXPERT_ROWS
    last = lambda r, be, nu, *_: jnp.maximum(jnp.minimum(r, nu[0] - 1), 0)
    exp3 = lambda r, be, nu, *_: (be[last(r, be, nu)], 0, 0)
    return pl.pallas_call(
        _expert_kernel,
        grid_spec=pltpu.PrefetchScalarGridSpec(
            num_scalar_prefetch=5,
            grid=(n_rows // rows,),
            in_specs=[pl.BlockSpec((rows, d // 2), lambda r, be, nu, *_: (last(r, be, nu), 0)),
                      pl.BlockSpec((1, 1, 2 * D_FF), exp3),
                      pl.BlockSpec((1, 1, d), exp3),
                      pl.BlockSpec(memory_space=pl.ANY),
                      pl.BlockSpec(memory_space=pl.ANY)],
            out_specs=pl.BlockSpec((rows, d), lambda r, *_: (r, 0)),
            scratch_shapes=[pltpu.VMEM((2, d, 2 * D_FF), F32),
                            pltpu.VMEM((2, D_FF, d), F32),
                            pltpu.VMEM((d, 2 * D_FF), BF16),
                            pltpu.VMEM((D_FF, d), BF16),
                            pltpu.SemaphoreType.DMA((2, 2))]),
        out_shape=jax.ShapeDtypeStruct((n_rows, d), F32),
        compiler_params=_params(1),
        name="experts",
    )(blk_e, n_used, blk_first, blk_slot, blk_next, xb, bup, bdn, wup, wdn)


def _combine_kernel(seg_ref, x1_ref, pos_ref, post_ref, gates_ref, mod_ref, g_ref, b_ref, yb_ref,
                    o_ref, ybuf_ref, sem):
    tc = x1_ref.shape[0]
    n_sorted = SORTED_ROWS
    ybuf_ref[TOP_K * tc:, :] = jnp.zeros((n_sorted - TOP_K * tc, ybuf_ref.shape[1]), F32)

    def make_copy(src, dst, size):
        return pltpu.make_async_copy(yb_ref.at[pl.ds(dst, size)], ybuf_ref.at[pl.ds(src, size)], sem)

    _segment_copies(seg_ref, make_copy, lambda cp: cp.start())

    pos = pos_ref[...]
    gates = gates_ref[...]
    grid = lax.broadcasted_iota(jnp.int32, (n_sorted, tc), 0)
    gsel = jnp.where(grid == pos[0:1, :], gates[0:1, :], 0.0)
    for k in range(1, TOP_K):
        gsel = gsel + jnp.where(grid == pos[k:k + 1, :], gates[k:k + 1, :], 0.0)
    gate_col = jnp.sum(gsel, axis=1, keepdims=True)
    unsort = _sort_matrix(post_ref[...], n_sorted, transposed=True)

    _segment_copies(seg_ref, make_copy, lambda cp: cp.wait())

    yg = ybuf_ref[...] * gate_col
    y_hi = yg.astype(BF16)
    y_lo = (yg - y_hi.astype(F32)).astype(BF16)
    h2 = _dot(unsort, y_hi) + _dot(unsort, y_lo)
    g2 = mod_ref[0, 5:6, :]
    o_ref[...] = _ln_plain(DEEPNORM_ALPHA * x1_ref[...] + g2 * h2) * g_ref[...] + b_ref[...]


def _combine_call(seg, x1_flat, pos, pos_t, gates, mod, g, b_, yb, seq_len):
    n_tok, d = x1_flat.shape
    tc = MOE_TILE
    c2 = lambda i: (0, 0)
    return pl.pallas_call(
        _combine_kernel,
        grid=(n_tok // tc,),
        in_specs=[pl.BlockSpec((1, N_EXPERTS, 3), lambda i: (i, 0, 0), memory_space=pltpu.SMEM),
                  pl.BlockSpec((tc, d), lambda i: (i, 0)),
                  pl.BlockSpec((TOP_K, tc), lambda i: (0, i)),
                  pl.BlockSpec((tc, TOP_K), lambda i: (i, 0)),
                  pl.BlockSpec((TOP_K, tc), lambda i: (0, i)),
                  pl.BlockSpec((1, 6, d), lambda i: ((i * tc) // seq_len, 0, 0)),
                  pl.BlockSpec((1, d), c2),
                  pl.BlockSpec((1, d), c2),
                  pl.BlockSpec(memory_space=pl.ANY)],
        out_specs=pl.BlockSpec((tc, d), lambda i: (i, 0)),
        out_shape=jax.ShapeDtypeStruct((n_tok, d), F32),
        scratch_shapes=[pltpu.VMEM((SORTED_ROWS, d), F32), pltpu.SemaphoreType.DMA],
        compiler_params=_params(1),
        name="combine",
    )(seg, x1_flat, pos, pos_t, gates, mod, g, b_, yb)


def _rope_tables(seq_len):
    inv_freq = ROPE_THETA ** (-jnp.arange(0, HEAD_DIM, 2, dtype=F32) / HEAD_DIM)
    ang = jnp.arange(seq_len, dtype=F32)[:, None] * inv_freq[None, :]
    cos, sin = lax.optimization_barrier((jnp.cos(ang), jnp.sin(ang)))
    zero = jnp.zeros_like(sin)
    reps = HEAD_W // HEAD_DIM
    cos_t = jnp.tile(jnp.concatenate([cos, cos], axis=1), (1, reps))
    sina_t = jnp.tile(jnp.concatenate([-sin, zero], axis=1), (1, reps))
    sinb_t = jnp.tile(jnp.concatenate([zero, sin], axis=1), (1, reps))
    return cos_t, sina_t, sinb_t


def _encode(x, mod, w):
    b, s, d = x.shape
    n_tok = b * s
    cos, sina, sinb = w["rope"]
    ua, q, k, vt, sga, sgb = _inproj_call(x, mod, cos, sina, sinb, w["wcat"], w["wvt"])
    yattn = _attn_call(q, k, vt, w["lq1"], w["lk1"], w["lq2"], w["lk2"], w["subw"])
    x1, pos, gates, seg, cnt = _mixer_call(
        x, ua, sga, sgb, yattn, mod, w["poolw"], w["pscale"], w["wa"], w["wb"], w["wout"],
        w["ln1_g"], w["ln1_b"], w["rwt"], w["rb"])

    rows = EXPERT_ROWS
    counts = cnt[:, 0].astype(jnp.int32)
    padded = (counts + rows - 1) // rows * rows
    pends = jnp.cumsum(padded)
    pstarts = pends - padded
    seg = seg.at[:, :, 2].add(pstarts[None, :])
    n_tiles = n_tok // MOE_TILE
    n_blocks = (n_tiles * SORTED_ROWS) // rows + N_EXPERTS
    blk_row0 = jnp.arange(n_blocks, dtype=jnp.int32) * rows
    blk_e = jnp.minimum(jnp.sum((pends[None, :] <= blk_row0[:, None]).astype(jnp.int32), axis=1),
                        N_EXPERTS - 1)
    n_used = (pends[-1:] // rows).astype(jnp.int32)
    blk_ids = jnp.arange(n_blocks, dtype=jnp.int32)
    blk_first = ((blk_ids == 0) | (blk_e != jnp.roll(blk_e, 1))).astype(jnp.int32)
    blk_slot = (jnp.cumsum(blk_first) - 1) % 2
    e_ids = jnp.arange(N_EXPERTS, dtype=jnp.int32)
    later = (e_ids[None, :] > e_ids[:, None]) & (padded[None, :] > 0)
    next_present = jnp.min(jnp.where(later, e_ids[None, :], N_EXPERTS), axis=1)
    next_present = jnp.where(next_present == N_EXPERTS, -1, next_present)
    blk_next = jnp.sum(jnp.where(blk_e[:, None] == e_ids[None, :], next_present[None, :], 0), axis=1)

    x1_flat = x1.reshape(n_tok, d)
    xb = _dispatch_call(pends.astype(jnp.int32), padded, n_used, seg, x1_flat, pos, mod, n_blocks * rows, s)
    yb = _expert_call(blk_e, n_used, blk_first, blk_slot.astype(jnp.int32), blk_next.astype(jnp.int32),
                      xb, w["wup"], w["bup"], w["wdn"], w["bdn"])
    out = _combine_call(seg, x1_flat, pos, pos.T, gates, mod, w["ln2_g"], w["ln2_b"], yb, s)
    return out.reshape(b, s, d)


def kernel(x_prompt, x_sample, c_prompt, c_sample, w_ada, b_ada, w_in, pool_w, pool_scale, w_branch_a, w_branch_b, lambda_q1, lambda_k1, lambda_q2, lambda_k2, subln_w, w_out, ln1_g, ln1_b, router_w, router_b, exp_w_up, exp_b_up, exp_w_down, exp_b_down, ln2_g, ln2_b):
    d = D_MODEL
    w_in0 = w_in[0]
    seg = lambda j: w_in0[:, j * d:(j + 1) * d]
    w = {
        "wcat": jnp.concatenate([seg(0), seg(1), seg(2), seg(4), seg(5)], axis=1).astype(BF16),
        "wvt": seg(3).T.astype(BF16),
        "lq1": lambda_q1, "lk1": lambda_k1, "lq2": lambda_q2, "lk2": lambda_k2,
        "subw": subln_w[0].reshape(HEAD_W, 1),
        "poolw": pool_w[0].astype(BF16),
        "pscale": pool_scale,
        "wa": w_branch_a[0].astype(BF16),
        "wb": w_branch_b[0].astype(BF16),
        "wout": w_out[0].astype(BF16),
        "ln1_g": ln1_g, "ln1_b": ln1_b,
        "rwt": router_w[0].T.astype(BF16),
        "rb": router_b[0].reshape(N_EXPERTS, 1),
        "wup": exp_w_up[0],
        "bup": exp_b_up[0].reshape(N_EXPERTS, 1, 2 * D_FF),
        "wdn": exp_w_down[0],
        "bdn": exp_b_down[0].reshape(N_EXPERTS, 1, d),
        "ln2_g": ln2_g, "ln2_b": ln2_b,
        "rope": _rope_tables(max(x_prompt.shape[1], x_sample.shape[1])),
    }
    nb_p, nb_s = c_prompt.shape[0], c_sample.shape[0]
    pad = (-(nb_p + nb_s)) % 8
    c_all = jnp.concatenate([c_prompt, c_sample, jnp.zeros((pad, d), F32)], axis=0)
    mod = _mod_call(c_all, w_ada[0], b_ada[0]).reshape(-1, 6, d)
    y_prompt = _encode(x_prompt, mod[:nb_p], w)
    y_sample = _encode(x_sample, mod[nb_p:nb_p + nb_s], w)
    return (y_prompt, y_sample)
```

```python
import functools

import jax
import jax.numpy as jnp
from jax import lax
from jax.experimental import pallas as pl
from jax.experimental.pallas import tpu as pltpu

F32 = jnp.float32
BF16 = jnp.bfloat16

D_MODEL = 1024
N_HEADS = 8
HEAD_DIM = 64
HEAD_W = 2 * HEAD_DIM
ROPE_THETA = 10000.0
POOL_WINDOWS = (2, 4, 8, 16)
POOL_GC = 256
POOL_HALO = 8
N_EXPERTS = 32
TOP_K = 4
D_FF = 1024
SWIGLU_LIMIT = 7.0
SWIGLU_ALPHA = 1.702
LN_EPS = 1e-5
DEPTH = 1
DEEPNORM_ALPHA = (2.0 * DEPTH) ** 0.25
LAMBDA_INIT = 0.8 - 0.6 * 1.0

VMEM_LIMIT_V7X = 56 * 1024 * 1024

TOKEN_TILE = 256
MOE_TILE = TOKEN_TILE
ATTN_Q_TILE = 1024
ATTN_K_CHUNK = 1024
ONES_ROWS = 16
LOG2_E = 1.4426950408889634
EXPERT_ROWS = 256


def _params(n_axes, **kw):
    return pltpu.CompilerParams(dimension_semantics=("arbitrary",) * n_axes,
                                vmem_limit_bytes=VMEM_LIMIT_V7X, **kw)


def _ln_plain(x):
    mu = jnp.mean(x, axis=-1, keepdims=True)
    xc = x - mu
    var = jnp.mean(xc * xc, axis=-1, keepdims=True)
    return xc * lax.rsqrt(var + LN_EPS)


def _dot(a, b):
    return jnp.dot(a, b, preferred_element_type=F32)


def _dot_nt(a, b, **kw):
    return lax.dot_general(a, b, (((1,), (1,)), ((), ())), preferred_element_type=F32, **kw)


def _mod_kernel(c_ref, w_ref, b_ref, o_ref):
    c = c_ref[...]
    s = c * jax.nn.sigmoid(c)
    o_ref[...] = jnp.dot(s, w_ref[...], preferred_element_type=F32,
                         precision=lax.Precision.HIGHEST) + b_ref[...]


def _mod_call(c, w_ada, b_ada):
    rows, d = c.shape
    n_out = w_ada.shape[1]
    return pl.pallas_call(
        _mod_kernel,
        grid=(n_out // d,),
        in_specs=[pl.BlockSpec((rows, d), lambda j: (0, 0)),
                  pl.BlockSpec((d, d), lambda j: (0, j)),
                  pl.BlockSpec((1, d), lambda j: (0, j))],
        out_specs=pl.BlockSpec((rows, d), lambda j: (0, j)),
        out_shape=jax.ShapeDtypeStruct((rows, n_out), F32),
        compiler_params=_params(1),
        name="mod",
    )(c, w_ada, b_ada.reshape(1, n_out))


def _inproj_kernel(x_ref, mod_ref, cos_ref, sina_ref, sinb_ref, wcat_ref, wvt_ref,
                   ua_ref, q_ref, k_ref, vt_ref, sga_ref, sgb_ref):
    d = D_MODEL
    x = x_ref[0]
    sh1 = mod_ref[0, 0:1, :]
    sc1 = mod_ref[0, 1:2, :]
    u = (_ln_plain(x) * (1.0 + sc1) + sh1).astype(BF16)

    def proj(j):
        return _dot(u, wcat_ref[:, j * d:(j + 1) * d])

    reps = d // HEAD_W
    cos = jnp.concatenate([cos_ref[...]] * reps, axis=1)
    sina = jnp.concatenate([sina_ref[...]] * reps, axis=1)
    sinb = jnp.concatenate([sinb_ref[...]] * reps, axis=1)

    def rope(t):
        half = HEAD_DIM // 2
        return t * cos + pltpu.roll(t, d - half, 1) * sina + pltpu.roll(t, half, 1) * sinb

    ua_ref[0] = proj(0)
    q_ref[0] = (rope(proj(1)) * (HEAD_DIM ** -0.5 * LOG2_E)).astype(BF16)
    k_ref[0] = rope(proj(2)).astype(BF16)
    vt_ref[0] = _dot_nt(wvt_ref[...], u).astype(BF16)
    sga_ref[0] = jax.nn.sigmoid(proj(3))
    sgb_ref[0] = jax.nn.sigmoid(proj(4))


def _inproj_call(x, mod, cos, sina, sinb, wcat, wvt):
    b, s, d = x.shape
    tm = min(TOKEN_TILE, s)
    row = lambda bi, i: (bi, i, 0)
    const2 = lambda bi, i: (0, 0)
    tab = pl.BlockSpec((tm, HEAD_W), lambda bi, i: (i, 0))
    return pl.pallas_call(
        _inproj_kernel,
        grid=(b, s // tm),
        in_specs=[pl.BlockSpec((1, tm, d), row),
                  pl.BlockSpec((1, 6, d), lambda bi, i: (bi, 0, 0)),
                  tab, tab, tab,
                  pl.BlockSpec(wcat.shape, const2),
                  pl.BlockSpec(wvt.shape, const2)],
        out_specs=[pl.BlockSpec((1, tm, d), row),
                   pl.BlockSpec((1, tm, d), row),
                   pl.BlockSpec((1, tm, d), row),
                   pl.BlockSpec((1, d, tm), lambda bi, i: (bi, 0, i)),
                   pl.BlockSpec((1, tm, d), row),
                   pl.BlockSpec((1, tm, d), row)],
        out_shape=[jax.ShapeDtypeStruct((b, s, d), F32),
                   jax.ShapeDtypeStruct((b, s, d), BF16),
                   jax.ShapeDtypeStruct((b, s, d), BF16),
                   jax.ShapeDtypeStruct((b, d, s), BF16),
                   jax.ShapeDtypeStruct((b, s, d), F32),
                   jax.ShapeDtypeStruct((b, s, d), F32)],
        compiler_params=_params(2),
        name="inproj",
    )(x, mod, cos, sina, sinb, wcat, wvt)


def _attn_kernel(q_ref, k_ref, vt_ref, lq1_ref, lk1_ref, lq2_ref, lk2_ref, subw_ref,
                 o_ref, qt_ref, sa_ref, sb_ref, o1_ref, o2_ref, *, k_chunk):
    tq = q_ref.shape[1]
    s_len = k_ref.shape[1]
    q = q_ref[0]
    lane = lax.broadcasted_iota(jnp.int32, q.shape, 1)
    zero = jnp.zeros_like(q)
    qt_ref[0] = jnp.where(lane < HEAD_DIM, q, zero).T
    qt_ref[1] = jnp.where(lane >= HEAD_DIM, q, zero).T
    o1_ref[...] = jnp.zeros_like(o1_ref)
    o2_ref[...] = jnp.zeros_like(o2_ref)
    n_chunks = s_len // k_chunk
    ones = jnp.ones((ONES_ROWS, k_chunk), BF16)

    def scores(c, s_ref):
        start = pl.multiple_of(c * k_chunk, k_chunk)
        kc = k_ref[0, pl.ds(start, k_chunk), :]
        s1 = _dot(kc, qt_ref[0])
        s2 = _dot(kc, qt_ref[1])
        s_ref[0] = s1
        s_ref[1] = s2
        return jnp.max(s1, axis=0, keepdims=True), jnp.max(s2, axis=0, keepdims=True)

    def accumulate(c, s_ref, cmax, m):
        start = pl.multiple_of(c * k_chunk, k_chunk)
        vtc = jnp.concatenate([vt_ref[0, :, pl.ds(start, k_chunk)], ones], axis=0)
        m_out = []
        for j, acc_ref in enumerate((o1_ref, o2_ref)):
            m_new = jnp.maximum(m[j], cmax[j])
            alpha = jnp.exp2(m[j] - m_new)
            p = jnp.exp2(s_ref[j] - m_new).astype(BF16)
            acc_ref[...] = alpha * acc_ref[...] + _dot(vtc, p)
            m_out.append(m_new)
        return tuple(m_out)

    def body(i, carry):
        cmax_a, m = carry
        cmax_b = scores(2 * i + 1, sb_ref)
        m = accumulate(2 * i, sa_ref, cmax_a, m)
        cmax_a = scores(2 * i + 2, sa_ref)
        m = accumulate(2 * i + 1, sb_ref, cmax_b, m)
        return cmax_a, m

    neg = jnp.full((1, tq), -jnp.inf, F32)
    cmax_a, m = lax.fori_loop(0, n_chunks // 2 - 1, body, (scores(0, sa_ref), (neg, neg)))
    cmax_b = scores(n_chunks - 1, sb_ref)
    m = accumulate(n_chunks - 2, sa_ref, cmax_a, m)
    accumulate(n_chunks - 1, sb_ref, cmax_b, m)

    lam = (jnp.exp(jnp.sum(lq1_ref[...] * lk1_ref[...], axis=1, keepdims=True))
           - jnp.exp(jnp.sum(lq2_ref[...] * lk2_ref[...], axis=1, keepdims=True))
           + LAMBDA_INIT)
    acc1 = o1_ref[...]
    acc2 = o2_ref[...]
    o = (acc1[:HEAD_W] / acc1[HEAD_W:HEAD_W + 1]
         - lam * (acc2[:HEAD_W] / acc2[HEAD_W:HEAD_W + 1]))
    o = o * lax.rsqrt(jnp.mean(o * o, axis=0, keepdims=True) + LN_EPS)
    o = o * subw_ref[...] * (1.0 - LAMBDA_INIT)
    o_ref[0] = o.T.astype(o_ref.dtype)


def _attn_call(q, k, vt, lq1, lk1, lq2, lk2, subw):
    b, s, d = q.shape
    tq = min(ATTN_Q_TILE, s)
    kc = min(ATTN_K_CHUNK, s // 4)
    assert s % (2 * kc) == 0 and s % tq == 0
    lam_spec = pl.BlockSpec((1, HEAD_DIM), lambda bi, h, i: (0, 0))
    return pl.pallas_call(
        functools.partial(_attn_kernel, k_chunk=kc),
        grid=(b, N_HEADS, s // tq),
        in_specs=[pl.BlockSpec((1, tq, HEAD_W), lambda bi, h, i: (bi, i, h)),
                  pl.BlockSpec((1, s, HEAD_W), lambda bi, h, i: (bi, 0, h)),
                  pl.BlockSpec((1, HEAD_W, s), lambda bi, h, i: (bi, h, 0)),
                  lam_spec, lam_spec, lam_spec, lam_spec,
                  pl.BlockSpec((HEAD_W, 1), lambda bi, h, i: (0, 0))],
        out_specs=pl.BlockSpec((1, tq, HEAD_W), lambda bi, h, i: (bi, i, h)),
        out_shape=jax.ShapeDtypeStruct((b, s, d), BF16),
        scratch_shapes=[pltpu.VMEM((2, HEAD_W, tq), BF16),
                        pltpu.VMEM((2, kc, tq), F32), pltpu.VMEM((2, kc, tq), F32),
                        pltpu.VMEM((HEAD_W + ONES_ROWS, tq), F32),
                        pltpu.VMEM((HEAD_W + ONES_ROWS, tq), F32)],
        compiler_params=_params(3),
        name="attn",
    )(q, k, vt, lq1, lk1, lq2, lk2, subw)


def _mixer_kernel(x_ref, ua_ref, uprev_ref, unext_ref, sga_ref, sgb_ref, yattn_ref, mod_ref,
                  poolw_ref, pscale_ref, wa_ref, wb_ref, wout_ref, g_ref, b_ref, rwt_ref, rb_ref,
                  x1_ref, pos_ref, gate_ref, seg_ref, cnt_ref,
                  ext_ref, run_ref, *, seq_len):
    tm = x_ref.shape[1]
    halo = POOL_HALO
    bi = pl.program_id(0)
    i = pl.program_id(1)
    n_i = pl.num_programs(1)

    @pl.when((bi == 0) & (i == 0))
    def _():
        run_ref[...] = jnp.zeros_like(run_ref)

    ua = ua_ref[0]
    ext_ref[0:halo, :] = jnp.where(i > 0, uprev_ref[0], 0.0)
    ext_ref[halo:halo + tm, :] = ua
    ext_ref[halo + tm:, :] = jnp.where(i < n_i - 1, unext_ref[0], 0.0)
    pos = i * tm + lax.broadcasted_iota(jnp.int32, (tm, 1), 0)
    mixed = []
    for g, w in enumerate(POOL_WINDOWS):
        cols = slice(g * POOL_GC, (g + 1) * POOL_GC)
        win = ext_ref[halo - w // 2:halo - w // 2 + tm, cols]
        for j in range(1, w):
            off = halo - w // 2 + j
            win = win + ext_ref[off:off + tm, cols]
        lo = jnp.maximum(pos - w // 2, 0)
        hi = jnp.minimum(pos + w // 2 - 1, seq_len - 1)
        cnt = (hi - lo + 1).astype(F32)
        pooled = win / cnt - ua[:, cols]
        mixed.append(_dot(pooled.astype(BF16), poolw_ref[g]))
    mixed = jnp.concatenate(mixed, axis=1) * pscale_ref[...]
    ya = _dot(mixed.astype(BF16), wa_ref[...])

    yb = _dot(yattn_ref[0], wb_ref[...])
    merged = sga_ref[0] * ya + sgb_ref[0] * yb
    h = _dot(merged.astype(BF16), wout_ref[...])
    g1 = mod_ref[0, 2:3, :]
    x1 = _ln_plain(DEEPNORM_ALPHA * x_ref[0] + g1 * h) * g_ref[...] + b_ref[...]
    x1_ref[0] = x1

    sh2 = mod_ref[0, 3:4, :]
    sc2 = mod_ref[0, 4:5, :]
    u2 = _ln_plain(x1) * (1.0 + sc2) + sh2
    logits = _dot_nt(rwt_ref[...], u2.astype(BF16)) + rb_ref[...]
    eio = lax.broadcasted_iota(jnp.int32, logits.shape, 0)
    vals, sels = [], []
    l = logits
    for _ in range(TOP_K):
        mv = jnp.max(l, axis=0, keepdims=True)
        idx = jnp.min(jnp.where(l == mv, eio, N_EXPERTS), axis=0, keepdims=True)
        sel = eio == idx
        vals.append(mv)
        sels.append(sel)
        l = jnp.where(sel, -jnp.inf, l)
    ex = [jnp.exp(v - vals[0]) for v in vals]
    den = ex[0] + ex[1] + ex[2] + ex[3]
    gate_ref[...] = jnp.concatenate([e / den for e in ex], axis=0)

    member = jnp.zeros(logits.shape, F32)
    for sel in sels:
        member = member + sel.astype(F32)
    member_b = member.astype(BF16)
    tok_before = (lax.broadcasted_iota(jnp.int32, (tm, tm), 0)
                  < lax.broadcasted_iota(jnp.int32, (tm, tm), 1)).astype(BF16)
    exp_before = (lax.broadcasted_iota(jnp.int32, (N_EXPERTS, N_EXPERTS), 1)
                  < lax.broadcasted_iota(jnp.int32, (N_EXPERTS, N_EXPERTS), 0)).astype(BF16)
    rank_in_tile = _dot(member_b, tok_before)
    seg_cnt = jnp.sum(member, axis=1, keepdims=True)
    seg_rows = jnp.ceil(seg_cnt * (1.0 / SEG_ALIGN)) * SEG_ALIGN
    seg_rows_b = jnp.broadcast_to(seg_rows, (N_EXPERTS, 128)).astype(BF16)
    seg_off = _dot(exp_before, seg_rows_b)[:, 0:1]
    base = seg_off + rank_in_tile
    pos = [jnp.sum(jnp.where(sel, base, 0.0), axis=0, keepdims=True) for sel in sels]
    pos_ref[...] = jnp.concatenate(pos, axis=0).astype(jnp.int32)
    run = run_ref[...]
    seg_ref[0] = jnp.concatenate([seg_rows, seg_off, run], axis=1).astype(jnp.int32)
    run_ref[...] = run + seg_rows
    cnt_ref[...] = jnp.broadcast_to(run + seg_rows, cnt_ref.shape)


def _mixer_call(x, ua, sga, sgb, yattn, mod, poolw, pscale, wa, wb, wout, g, b_, rwt, rb):
    b, s, d = x.shape
    tm = min(TOKEN_TILE, s)
    n_i = s // tm
    hb = tm // POOL_HALO
    n_hb = s // POOL_HALO
    row = lambda bi, i: (bi, i, 0)
    c2 = lambda bi, i: (0, 0)
    c3 = lambda bi, i: (0, 0, 0)
    tokcol = lambda bi, i: (0, bi * n_i + i)
    n_tok = b * s
    return pl.pallas_call(
        functools.partial(_mixer_kernel, seq_len=s),
        grid=(b, n_i),
        in_specs=[pl.BlockSpec((1, tm, d), row),
                  pl.BlockSpec((1, tm, d), row),
                  pl.BlockSpec((1, POOL_HALO, d), lambda bi, i: (bi, jnp.maximum(i * hb - 1, 0), 0)),
                  pl.BlockSpec((1, POOL_HALO, d), lambda bi, i: (bi, jnp.minimum((i + 1) * hb, n_hb - 1), 0)),
                  pl.BlockSpec((1, tm, d), row),
                  pl.BlockSpec((1, tm, d), row),
                  pl.BlockSpec((1, tm, d), row),
                  pl.BlockSpec((1, 6, d), lambda bi, i: (bi, 0, 0)),
                  pl.BlockSpec(poolw.shape, c3),
                  pl.BlockSpec((1, d), c2),
                  pl.BlockSpec((d, d), c2),
                  pl.BlockSpec((d, d), c2),
                  pl.BlockSpec((d, d), c2),
                  pl.BlockSpec((1, d), c2),
                  pl.BlockSpec((1, d), c2),
                  pl.BlockSpec((N_EXPERTS, d), c2),
                  pl.BlockSpec((N_EXPERTS, 1), c2)],
        out_specs=[pl.BlockSpec((1, tm, d), row),
                   pl.BlockSpec((TOP_K, tm), tokcol),
                   pl.BlockSpec((TOP_K, tm), tokcol),
                   pl.BlockSpec((1, N_EXPERTS, 3), lambda bi, i: (bi * n_i + i, 0, 0)),
                   pl.BlockSpec((N_EXPERTS, 128), c2)],
        out_shape=[jax.ShapeDtypeStruct((b, s, d), F32),
                   jax.ShapeDtypeStruct((TOP_K, n_tok), jnp.int32),
                   jax.ShapeDtypeStruct((TOP_K, n_tok), F32),
                   jax.ShapeDtypeStruct((b * n_i, N_EXPERTS, 3), jnp.int32),
                   jax.ShapeDtypeStruct((N_EXPERTS, 128), F32)],
        scratch_shapes=[pltpu.VMEM((tm + 2 * POOL_HALO, d), F32),
                        pltpu.VMEM((N_EXPERTS, 1), F32)],
        compiler_params=_params(2),
        name="mixer",
    )(x, ua, ua, ua, sga, sgb, yattn, mod, poolw, pscale, wa, wb, wout, g, b_, rwt, rb)


SEG_ALIGN = 8
SORTED_ROWS = TOP_K * MOE_TILE + SEG_ALIGN * N_EXPERTS
SEG_BITS = tuple(range(MOE_TILE.bit_length() - 1, SEG_ALIGN.bit_length() - 2, -1))


def _segment_copies(seg_ref, make_copy, action):
    def body(e, c):
        cnt = seg_ref[0, e, 0]
        src = seg_ref[0, e, 1]
        dst = seg_ref[0, e, 2]
        @pl.when(cnt > 0)
        def _():
            action(make_copy(pl.multiple_of(src, SEG_ALIGN), pl.multiple_of(dst, SEG_ALIGN),
                             pl.multiple_of(cnt, SEG_ALIGN)))
        return c

    lax.fori_loop(0, N_EXPERTS, body, 0)


def _sort_matrix(pos, n_sorted, transposed):
    if transposed:
        grid = lax.broadcasted_iota(jnp.int32, (pos.shape[0], n_sorted), 1)
        hit = grid == pos[:, 0:1]
        for k in range(1, TOP_K):
            hit = hit | (grid == pos[:, k:k + 1])
    else:
        grid = lax.broadcasted_iota(jnp.int32, (n_sorted, pos.shape[1]), 0)
        hit = grid == pos[0:1, :]
        for k in range(1, TOP_K):
            hit = hit | (grid == pos[k:k + 1, :])
    return jnp.where(hit, 1.0, 0.0).astype(BF16)


def _dispatch_kernel(pends_ref, padded_ref, n_used_ref,
                     seg_ref, x1_ref, pos_ref, mod_ref, xb_ref, sorted_ref, zero_ref, sem):
    td, d = x1_ref.shape
    n_blocks = xb_ref.shape[0] // EXPERT_ROWS

    @pl.when(pl.program_id(0) == 0)
    def _():
        zero_ref[...] = jnp.zeros_like(zero_ref)

        def zero_copy(row0):
            return pltpu.make_async_copy(
                zero_ref, xb_ref.at[pl.ds(pl.multiple_of(row0, EXPERT_ROWS), EXPERT_ROWS)], sem)

        def zero_blocks(action):
            def per_expert(e, c):
                @pl.when(padded_ref[e] > 0)
                def _():
                    action(zero_copy(pends_ref[e] - EXPERT_ROWS))
                return c

            def per_unused(r, c):
                action(zero_copy(r * EXPERT_ROWS))
                return c

            lax.fori_loop(0, N_EXPERTS, per_expert, 0)
            lax.fori_loop(n_used_ref[0], n_blocks, per_unused, 0)

        zero_blocks(lambda cp: cp.start())
        zero_blocks(lambda cp: cp.wait())

    sh2 = mod_ref[0, 3:4, :]
    sc2 = mod_ref[0, 4:5, :]
    u = (_ln_plain(x1_ref[...]) * (1.0 + sc2) + sh2).astype(BF16)
    rows = _dot(_sort_matrix(pos_ref[...], SORTED_ROWS, transposed=False), u)
    lo = lax.bitcast_convert_type(rows[:, :d // 2], jnp.uint32)
    hi = lax.bitcast_convert_type(rows[:, d // 2:], jnp.uint32)
    sorted_ref[...] = hi | (lo >> 16)

    def make_copy(src, dst, size):
        return pltpu.make_async_copy(sorted_ref.at[pl.ds(src, size)], xb_ref.at[pl.ds(dst, size)], sem)

    _segment_copies(seg_ref, make_copy, lambda cp: cp.start())
    _segment_copies(seg_ref, make_copy, lambda cp: cp.wait())


def _dispatch_call(pends, padded, n_used, seg, x1_flat, pos, mod, n_rows, seq_len):
    n_tok, d = x1_flat.shape
    td = MOE_TILE
    return pl.pallas_call(
        _dispatch_kernel,
        grid_spec=pltpu.PrefetchScalarGridSpec(
            num_scalar_prefetch=3,
            grid=(n_tok // td,),
            in_specs=[pl.BlockSpec((1, N_EXPERTS, 3), lambda i, *_: (i, 0, 0), memory_space=pltpu.SMEM),
                      pl.BlockSpec((td, d), lambda i, *_: (i, 0)),
                      pl.BlockSpec((TOP_K, td), lambda i, *_: (0, i)),
                      pl.BlockSpec((1, 6, d), lambda i, *_: ((i * td) // seq_len, 0, 0))],
            out_specs=pl.BlockSpec(memory_space=pl.ANY),
            scratch_shapes=[pltpu.VMEM((SORTED_ROWS, d // 2), jnp.uint32),
                            pltpu.VMEM((EXPERT_ROWS, d // 2), jnp.uint32),
                            pltpu.SemaphoreType.DMA]),
        out_shape=jax.ShapeDtypeStruct((n_rows, d // 2), jnp.uint32),
        compiler_params=_params(1, has_side_effects=True),
        name="dispatch",
    )(pends, padded, n_used, seg, x1_flat, pos, mod)


CAST_ROWS = 128


def _expert_kernel(blk_e_ref, n_used_ref, first_ref, slot_ref, next_ref,
                   xb_ref, bup_ref, bdn_ref, wup_hbm, wdn_hbm, yb_ref,
                   wup32_ref, wdn32_ref, wup_ref, wdn_ref, sems):
    r = pl.program_id(0)
    used = r < n_used_ref[0]

    def weight_copies(e, slot):
        return (pltpu.make_async_copy(wup_hbm.at[e], wup32_ref.at[slot], sems.at[slot, 0]),
                pltpu.make_async_copy(wdn_hbm.at[e], wdn32_ref.at[slot], sems.at[slot, 1]))

    @pl.when(jnp.logical_not(used))
    def _():
        yb_ref[...] = jnp.zeros_like(yb_ref)

    @pl.when(used & (first_ref[r] == 1))
    def _():
        slot = slot_ref[r]

        @pl.when(r == 0)
        def _():
            for cp in weight_copies(blk_e_ref[r], slot):
                cp.start()

        for cp in weight_copies(blk_e_ref[r], slot):
            cp.wait()
        nxt = next_ref[r]

        @pl.when(nxt >= 0)
        def _():
            for cp in weight_copies(nxt, 1 - slot):
                cp.start()

        def cast(i, c):
            rows_ = pl.ds(pl.multiple_of(i * CAST_ROWS, CAST_ROWS), CAST_ROWS)
            wup_ref[rows_, :] = wup32_ref[slot, rows_, :].astype(BF16)
            wdn_ref[rows_, :] = wdn32_ref[slot, rows_, :].astype(BF16)
            return c

        lax.fori_loop(0, D_MODEL // CAST_ROWS, cast, 0)

    @pl.when(used)
    def _():
        words = xb_ref[...]
        lo = lax.bitcast_convert_type(words << 16, F32).astype(BF16)
        hi = lax.bitcast_convert_type(words & jnp.uint32(0xFFFF0000), F32).astype(BF16)
        half = D_MODEL // 2
        h = _dot(lo, wup_ref[:half, :]) + _dot(hi, wup_ref[half:, :]) + bup_ref[0]
        gate = jnp.minimum(h[:, :D_FF], SWIGLU_LIMIT)
        up = jnp.clip(h[:, D_FF:], -SWIGLU_LIMIT, SWIGLU_LIMIT)
        act = (up + 1.0) * (gate * jax.nn.sigmoid(SWIGLU_ALPHA * gate))
        yb_ref[...] = _dot(act.astype(BF16), wdn_ref[...]) + bdn_ref[0]


def _expert_call(blk_e, n_used, blk_first, blk_slot, blk_next, xb, wup, bup, wdn, bdn):
    n_rows = xb.shape[0]
    d = D_MODEL
    assert D_FF == d
    rows = EXPERT_ROWS
    last = lambda r, be, nu, *_: jnp.maximum(jnp.minimum(r, nu[0] - 1), 0)
    exp3 = lambda r, be, nu, *_: (be[last(r, be, nu)], 0, 0)
    return pl.pallas_call(
        _expert_kernel,
        grid_spec=pltpu.PrefetchScalarGridSpec(
            num_scalar_prefetch=5,
            grid=(n_rows // rows,),
            in_specs=[pl.BlockSpec((rows, d // 2), lambda r, be, nu, *_: (last(r, be, nu), 0)),
                      pl.BlockSpec((1, 1, 2 * D_FF), exp3),
                      pl.BlockSpec((1, 1, d), exp3),
                      pl.BlockSpec(memory_space=pl.ANY),
                      pl.BlockSpec(memory_space=pl.ANY)],
            out_specs=pl.BlockSpec((rows, d), lambda r, *_: (r, 0)),
            scratch_shapes=[pltpu.VMEM((2, d, 2 * D_FF), F32),
                            pltpu.VMEM((2, D_FF, d), F32),
                            pltpu.VMEM((d, 2 * D_FF), BF16),
                            pltpu.VMEM((D_FF, d), BF16),
                            pltpu.SemaphoreType.DMA((2, 2))]),
        out_shape=jax.ShapeDtypeStruct((n_rows, d), F32),
        compiler_params=_params(1),
        name="experts",
    )(blk_e, n_used, blk_first, blk_slot, blk_next, xb, bup, bdn, wup, wdn)


def _combine_kernel(seg_ref, x1_ref, pos_ref, post_ref, gates_ref, mod_ref, g_ref, b_ref, yb_ref,
                    o_ref, ybuf_ref, sem):
    tc = x1_ref.shape[0]
    n_sorted = SORTED_ROWS
    ybuf_ref[TOP_K * tc:, :] = jnp.zeros((n_sorted - TOP_K * tc, ybuf_ref.shape[1]), F32)

    def make_copy(src, dst, size):
        return pltpu.make_async_copy(yb_ref.at[pl.ds(dst, size)], ybuf_ref.at[pl.ds(src, size)], sem)

    _segment_copies(seg_ref, make_copy, lambda cp: cp.start())

    pos = pos_ref[...]
    gates = gates_ref[...]
    grid = lax.broadcasted_iota(jnp.int32, (n_sorted, tc), 0)
    gsel = jnp.where(grid == pos[0:1, :], gates[0:1, :], 0.0)
    for k in range(1, TOP_K):
        gsel = gsel + jnp.where(grid == pos[k:k + 1, :], gates[k:k + 1, :], 0.0)
    gate_col = jnp.sum(gsel, axis=1, keepdims=True)
    unsort = _sort_matrix(post_ref[...], n_sorted, transposed=True)

    _segment_copies(seg_ref, make_copy, lambda cp: cp.wait())

    yg = ybuf_ref[...] * gate_col
    y_hi = yg.astype(BF16)
    y_lo = (yg - y_hi.astype(F32)).astype(BF16)
    h2 = _dot(unsort, y_hi) + _dot(unsort, y_lo)
    g2 = mod_ref[0, 5:6, :]
    o_ref[...] = _ln_plain(DEEPNORM_ALPHA * x1_ref[...] + g2 * h2) * g_ref[...] + b_ref[...]


def _combine_call(seg, x1_flat, pos, pos_t, gates, mod, g, b_, yb, seq_len):
    n_tok, d = x1_flat.shape
    tc = MOE_TILE
    c2 = lambda i: (0, 0)
    return pl.pallas_call(
        _combine_kernel,
        grid=(n_tok // tc,),
        in_specs=[pl.BlockSpec((1, N_EXPERTS, 3), lambda i: (i, 0, 0), memory_space=pltpu.SMEM),
                  pl.BlockSpec((tc, d), lambda i: (i, 0)),
                  pl.BlockSpec((TOP_K, tc), lambda i: (0, i)),
                  pl.BlockSpec((tc, TOP_K), lambda i: (i, 0)),
                  pl.BlockSpec((TOP_K, tc), lambda i: (0, i)),
                  pl.BlockSpec((1, 6, d), lambda i: ((i * tc) // seq_len, 0, 0)),
                  pl.BlockSpec((1, d), c2),
                  pl.BlockSpec((1, d), c2),
                  pl.BlockSpec(memory_space=pl.ANY)],
        out_specs=pl.BlockSpec((tc, d), lambda i: (i, 0)),
        out_shape=jax.ShapeDtypeStruct((n_tok, d), F32),
        scratch_shapes=[pltpu.VMEM((SORTED_ROWS, d), F32), pltpu.SemaphoreType.DMA],
        compiler_params=_params(1),
        name="combine",
    )(seg, x1_flat, pos, pos_t, gates, mod, g, b_, yb)


def _rope_tables(seq_len):
    inv_freq = ROPE_THETA ** (-jnp.arange(0, HEAD_DIM, 2, dtype=F32) / HEAD_DIM)
    ang = jnp.arange(seq_len, dtype=F32)[:, None] * inv_freq[None, :]
    cos, sin = lax.optimization_barrier((jnp.cos(ang), jnp.sin(ang)))
    zero = jnp.zeros_like(sin)
    reps = HEAD_W // HEAD_DIM
    cos_t = jnp.tile(jnp.concatenate([cos, cos], axis=1), (1, reps))
    sina_t = jnp.tile(jnp.concatenate([-sin, zero], axis=1), (1, reps))
    sinb_t = jnp.tile(jnp.concatenate([zero, sin], axis=1), (1, reps))
    return cos_t, sina_t, sinb_t


def _encode(x, mod, w):
    b, s, d = x.shape
    n_tok = b * s
    cos, sina, sinb = w["rope"]
    ua, q, k, vt, sga, sgb = _inproj_call(x, mod, cos, sina, sinb, w["wcat"], w["wvt"])
    yattn = _attn_call(q, k, vt, w["lq1"], w["lk1"], w["lq2"], w["lk2"], w["subw"])
    x1, pos, gates, seg, cnt = _mixer_call(
        x, ua, sga, sgb, yattn, mod, w["poolw"], w["pscale"], w["wa"], w["wb"], w["wout"],
        w["ln1_g"], w["ln1_b"], w["rwt"], w["rb"])

    rows = EXPERT_ROWS
    counts = cnt[:, 0].astype(jnp.int32)
    padded = (counts + rows - 1) // rows * rows
    pends = jnp.cumsum(padded)
    pstarts = pends - padded
    seg = seg.at[:, :, 2].add(pstarts[None, :])
    n_tiles = n_tok // MOE_TILE
    n_blocks = (n_tiles * SORTED_ROWS) // rows + N_EXPERTS
    blk_row0 = jnp.arange(n_blocks, dtype=jnp.int32) * rows
    blk_e = jnp.minimum(jnp.sum((pends[None, :] <= blk_row0[:, None]).astype(jnp.int32), axis=1),
                        N_EXPERTS - 1)
    n_used = (pends[-1:] // rows).astype(jnp.int32)
    blk_ids = jnp.arange(n_blocks, dtype=jnp.int32)
    blk_first = ((blk_ids == 0) | (blk_e != jnp.roll(blk_e, 1))).astype(jnp.int32)
    blk_slot = (jnp.cumsum(blk_first) - 1) % 2
    e_ids = jnp.arange(N_EXPERTS, dtype=jnp.int32)
    later = (e_ids[None, :] > e_ids[:, None]) & (padded[None, :] > 0)
    next_present = jnp.min(jnp.where(later, e_ids[None, :], N_EXPERTS), axis=1)
    next_present = jnp.where(next_present == N_EXPERTS, -1, next_present)
    blk_next = jnp.sum(jnp.where(blk_e[:, None] == e_ids[None, :], next_present[None, :], 0), axis=1)

    x1_flat = x1.reshape(n_tok, d)
    xb = _dispatch_call(pends.astype(jnp.int32), padded, n_used, seg, x1_flat, pos, mod, n_blocks * rows, s)
    yb = _expert_call(blk_e, n_used, blk_first, blk_slot.astype(jnp.int32), blk_next.astype(jnp.int32),
                      xb, w["wup"], w["bup"], w["wdn"], w["bdn"])
    out = _combine_call(seg, x1_flat, pos, pos.T, gates, mod, w["ln2_g"], w["ln2_b"], yb, s)
    return out.reshape(b, s, d)


def kernel(x_prompt, x_sample, c_prompt, c_sample, w_ada, b_ada, w_in, pool_w, pool_scale, w_branch_a, w_branch_b, lambda_q1, lambda_k1, lambda_q2, lambda_k2, subln_w, w_out, ln1_g, ln1_b, router_w, router_b, exp_w_up, exp_b_up, exp_w_down, exp_b_down, ln2_g, ln2_b):
    d = D_MODEL
    w_in0 = w_in[0]
    seg = lambda j: w_in0[:, j * d:(j + 1) * d]
    w = {
        "wcat": jnp.concatenate([seg(0), seg(1), seg(2), seg(4), seg(5)], axis=1).astype(BF16),
        "wvt": seg(3).T.astype(BF16),
        "lq1": lambda_q1, "lk1": lambda_k1, "lq2": lambda_q2, "lk2": lambda_k2,
        "subw": subln_w[0].reshape(HEAD_W, 1),
        "poolw": pool_w[0].astype(BF16),
        "pscale": pool_scale,
        "wa": w_branch_a[0].astype(BF16),
        "wb": w_branch_b[0].astype(BF16),
        "wout": w_out[0].astype(BF16),
        "ln1_g": ln1_g, "ln1_b": ln1_b,
        "rwt": router_w[0].T.astype(BF16),
        "rb": router_b[0].reshape(N_EXPERTS, 1),
        "wup": exp_w_up[0],
        "bup": exp_b_up[0].reshape(N_EXPERTS, 1, 2 * D_FF),
        "wdn": exp_w_down[0],
        "bdn": exp_b_down[0].reshape(N_EXPERTS, 1, d),
        "ln2_g": ln2_g, "ln2_b": ln2_b,
        "rope": _rope_tables(max(x_prompt.shape[1], x_sample.shape[1])),
    }
    nb_p, nb_s = c_prompt.shape[0], c_sample.shape[0]
    pad = (-(nb_p + nb_s)) % 8
    c_all = jnp.concatenate([c_prompt, c_sample, jnp.zeros((pad, d), F32)], axis=0)
    mod = _mod_call(c_all, w_ada[0], b_ada[0]).reshape(-1, 6, d)
    y_prompt = _encode(x_prompt, mod[:nb_p], w)
    y_sample = _encode(x_sample, mod[nb_p:nb_p + nb_s], w)
    return (y_prompt, y_sample)
```

```python
import functools

import jax
import jax.numpy as jnp
from jax import lax
from jax.experimental import pallas as pl
from jax.experimental.pallas import tpu as pltpu

F32 = jnp.float32
BF16 = jnp.bfloat16

D_MODEL = 1024
N_HEADS = 8
HEAD_DIM = 64
HEAD_W = 2 * HEAD_DIM
ROPE_THETA = 10000.0
POOL_WINDOWS = (2, 4, 8, 16)
POOL_GC = 256
POOL_HALO = 8
N_EXPERTS = 32
TOP_K = 4
D_FF = 1024
SWIGLU_LIMIT = 7.0
SWIGLU_ALPHA = 1.702
LN_EPS = 1e-5
DEPTH = 1
DEEPNORM_ALPHA = (2.0 * DEPTH) ** 0.25
LAMBDA_INIT = 0.8 - 0.6 * 1.0

VMEM_LIMIT_V7X = 56 * 1024 * 1024

TOKEN_TILE = 256
MOE_TILE = TOKEN_TILE
ATTN_Q_TILE = 1024
ATTN_K_CHUNK = 1024
ONES_ROWS = 16
LOG2_E = 1.4426950408889634
EXPERT_ROWS = 256


def _params(n_axes, **kw):
    return pltpu.CompilerParams(dimension_semantics=("arbitrary",) * n_axes,
                                vmem_limit_bytes=VMEM_LIMIT_V7X, **kw)


def _ln_plain(x):
    mu = jnp.mean(x, axis=-1, keepdims=True)
    xc = x - mu
    var = jnp.mean(xc * xc, axis=-1, keepdims=True)
    return xc * lax.rsqrt(var + LN_EPS)


def _dot(a, b):
    return jnp.dot(a, b, preferred_element_type=F32)


def _dot_nt(a, b, **kw):
    return lax.dot_general(a, b, (((1,), (1,)), ((), ())), preferred_element_type=F32, **kw)


def _mod_kernel(c_ref, w_ref, b_ref, o_ref):
    c = c_ref[...]
    s = c * jax.nn.sigmoid(c)
    o_ref[...] = jnp.dot(s, w_ref[...], preferred_element_type=F32,
                         precision=lax.Precision.HIGHEST) + b_ref[...]


def _mod_call(c, w_ada, b_ada):
    rows, d = c.shape
    n_out = w_ada.shape[1]
    return pl.pallas_call(
        _mod_kernel,
        grid=(n_out // d,),
        in_specs=[pl.BlockSpec((rows, d), lambda j: (0, 0)),
                  pl.BlockSpec((d, d), lambda j: (0, j)),
                  pl.BlockSpec((1, d), lambda j: (0, j))],
        out_specs=pl.BlockSpec((rows, d), lambda j: (0, j)),
        out_shape=jax.ShapeDtypeStruct((rows, n_out), F32),
        compiler_params=_params(1),
        name="mod",
    )(c, w_ada, b_ada.reshape(1, n_out))


def _inproj_kernel(x_ref, mod_ref, cos_ref, sina_ref, sinb_ref, wcat_ref, wvt_ref,
                   ua_ref, q_ref, k_ref, vt_ref, sga_ref, sgb_ref):
    d = D_MODEL
    x = x_ref[0]
    sh1 = mod_ref[0, 0:1, :]
    sc1 = mod_ref[0, 1:2, :]
    u = (_ln_plain(x) * (1.0 + sc1) + sh1).astype(BF16)

    def proj(j):
        return _dot(u, wcat_ref[:, j * d:(j + 1) * d])

    reps = d // HEAD_W
    cos = jnp.concatenate([cos_ref[...]] * reps, axis=1)
    sina = jnp.concatenate([sina_ref[...]] * reps, axis=1)
    sinb = jnp.concatenate([sinb_ref[...]] * reps, axis=1)

    def rope(t):
        half = HEAD_DIM // 2
        return t * cos + pltpu.roll(t, d - half, 1) * sina + pltpu.roll(t, half, 1) * sinb

    ua_ref[0] = proj(0)
    q_ref[0] = (rope(proj(1)) * (HEAD_DIM ** -0.5 * LOG2_E)).astype(BF16)
    k_ref[0] = rope(proj(2)).astype(BF16)
    vt_ref[0] = _dot_nt(wvt_ref[...], u).astype(BF16)
    sga_ref[0] = jax.nn.sigmoid(proj(3))
    sgb_ref[0] = jax.nn.sigmoid(proj(4))


def _inproj_call(x, mod, cos, sina, sinb, wcat, wvt):
    b, s, d = x.shape
    tm = min(TOKEN_TILE, s)
    row = lambda bi, i: (bi, i, 0)
    const2 = lambda bi, i: (0, 0)
    tab = pl.BlockSpec((tm, HEAD_W), lambda bi, i: (i, 0))
    return pl.pallas_call(
        _inproj_kernel,
        grid=(b, s // tm),
        in_specs=[pl.BlockSpec((1, tm, d), row),
                  pl.BlockSpec((1, 6, d), lambda bi, i: (bi, 0, 0)),
                  tab, tab, tab,
                  pl.BlockSpec(wcat.shape, const2),
                  pl.BlockSpec(wvt.shape, const2)],
        out_specs=[pl.BlockSpec((1, tm, d), row),
                   pl.BlockSpec((1, tm, d), row),
                   pl.BlockSpec((1, tm, d), row),
                   pl.BlockSpec((1, d, tm), lambda bi, i: (bi, 0, i)),
                   pl.BlockSpec((1, tm, d), row),
                   pl.BlockSpec((1, tm, d), row)],
        out_shape=[jax.ShapeDtypeStruct((b, s, d), F32),
                   jax.ShapeDtypeStruct((b, s, d), BF16),
                   jax.ShapeDtypeStruct((b, s, d), BF16),
                   jax.ShapeDtypeStruct((b, d, s), BF16),
                   jax.ShapeDtypeStruct((b, s, d), F32),
                   jax.ShapeDtypeStruct((b, s, d), F32)],
        compiler_params=_params(2),
        name="inproj",
    )(x, mod, cos, sina, sinb, wcat, wvt)


def _attn_kernel(q_ref, k_ref, vt_ref, lq1_ref, lk1_ref, lq2_ref, lk2_ref, subw_ref,
                 o_ref, qt_ref, sa_ref, sb_ref, o1_ref, o2_ref, *, k_chunk):
    tq = q_ref.shape[1]
    s_len = k_ref.shape[1]
    q = q_ref[0]
    lane = lax.broadcasted_iota(jnp.int32, q.shape, 1)
    zero = jnp.zeros_like(q)
    qt_ref[0] = jnp.where(lane < HEAD_DIM, q, zero).T
    qt_ref[1] = jnp.where(lane >= HEAD_DIM, q, zero).T
    o1_ref[...] = jnp.zeros_like(o1_ref)
    o2_ref[...] = jnp.zeros_like(o2_ref)
    n_chunks = s_len // k_chunk
    ones = jnp.ones((ONES_ROWS, k_chunk), BF16)

    def scores(c, s_ref):
        start = pl.multiple_of(c * k_chunk, k_chunk)
        kc = k_ref[0, pl.ds(start, k_chunk), :]
        s1 = _dot(kc, qt_ref[0])
        s2 = _dot(kc, qt_ref[1])
        s_ref[0] = s1
        s_ref[1] = s2
        return jnp.max(s1, axis=0, keepdims=True), jnp.max(s2, axis=0, keepdims=True)

    def accumulate(c, s_ref, cmax, m):
        start = pl.multiple_of(c * k_chunk, k_chunk)
        vtc = jnp.concatenate([vt_ref[0, :, pl.ds(start, k_chunk)], ones], axis=0)
        m_out = []
        for j, acc_ref in enumerate((o1_ref, o2_ref)):
            m_new = jnp.maximum(m[j], cmax[j])
            alpha = jnp.exp2(m[j] - m_new)
            p = jnp.exp2(s_ref[j] - m_new).astype(BF16)
            acc_ref[...] = alpha * acc_ref[...] + _dot(vtc, p)
            m_out.append(m_new)
        return tuple(m_out)

    def body(i, carry):
        cmax_a, m = carry
        cmax_b = scores(2 * i + 1, sb_ref)
        m = accumulate(2 * i, sa_ref, cmax_a, m)
        cmax_a = scores(2 * i + 2, sa_ref)
        m = accumulate(2 * i + 1, sb_ref, cmax_b, m)
        return cmax_a, m

    neg = jnp.full((1, tq), -jnp.inf, F32)
    cmax_a, m = lax.fori_loop(0, n_chunks // 2 - 1, body, (scores(0, sa_ref), (neg, neg)))
    cmax_b = scores(n_chunks - 1, sb_ref)
    m = accumulate(n_chunks - 2, sa_ref, cmax_a, m)
    accumulate(n_chunks - 1, sb_ref, cmax_b, m)

    lam = (jnp.exp(jnp.sum(lq1_ref[...] * lk1_ref[...], axis=1, keepdims=True))
           - jnp.exp(jnp.sum(lq2_ref[...] * lk2_ref[...], axis=1, keepdims=True))
           + LAMBDA_INIT)
    acc1 = o1_ref[...]
    acc2 = o2_ref[...]
    o = (acc1[:HEAD_W] / acc1[HEAD_W:HEAD_W + 1]
         - lam * (acc2[:HEAD_W] / acc2[HEAD_W:HEAD_W + 1]))
    o = o * lax.rsqrt(jnp.mean(o * o, axis=0, keepdims=True) + LN_EPS)
    o = o * subw_ref[...] * (1.0 - LAMBDA_INIT)
    o_ref[0] = o.T.astype(o_ref.dtype)


def _attn_call(q, k, vt, lq1, lk1, lq2, lk2, subw):
    b, s, d = q.shape
    tq = min(ATTN_Q_TILE, s)
    kc = min(ATTN_K_CHUNK, s // 4)
    assert s % (2 * kc) == 0 and s % tq == 0
    lam_spec = pl.BlockSpec((1, HEAD_DIM), lambda bi, h, i: (0, 0))
    return pl.pallas_call(
        functools.partial(_attn_kernel, k_chunk=kc),
        grid=(b, N_HEADS, s // tq),
        in_specs=[pl.BlockSpec((1, tq, HEAD_W), lambda bi, h, i: (bi, i, h)),
                  pl.BlockSpec((1, s, HEAD_W), lambda bi, h, i: (bi, 0, h)),
                  pl.BlockSpec((1, HEAD_W, s), lambda bi, h, i: (bi, h, 0)),
                  lam_spec, lam_spec, lam_spec, lam_spec,
                  pl.BlockSpec((HEAD_W, 1), lambda bi, h, i: (0, 0))],
        out_specs=pl.BlockSpec((1, tq, HEAD_W), lambda bi, h, i: (bi, i, h)),
        out_shape=jax.ShapeDtypeStruct((b, s, d), BF16),
        scratch_shapes=[pltpu.VMEM((2, HEAD_W, tq), BF16),
                        pltpu.VMEM((2, kc, tq), F32), pltpu.VMEM((2, kc, tq), F32),
                        pltpu.VMEM((HEAD_W + ONES_ROWS, tq), F32),
                        pltpu.VMEM((HEAD_W + ONES_ROWS, tq), F32)],
        compiler_params=_params(3),
        name="attn",
    )(q, k, vt, lq1, lk1, lq2, lk2, subw)


def _mixer_kernel(x_ref, ua_ref, uprev_ref, unext_ref, sga_ref, sgb_ref, yattn_ref, mod_ref,
                  poolw_ref, pscale_ref, wa_ref, wb_ref, wout_ref, g_ref, b_ref, rwt_ref, rb_ref,
                  x1_ref, pos_ref, gate_ref, seg_ref, cnt_ref,
                  ext_ref, run_ref, *, seq_len):
    tm = x_ref.shape[1]
    halo = POOL_HALO
    bi = pl.program_id(0)
    i = pl.program_id(1)
    n_i = pl.num_programs(1)

    @pl.when((bi == 0) & (i == 0))
    def _():
        run_ref[...] = jnp.zeros_like(run_ref)

    ua = ua_ref[0]
    ext_ref[0:halo, :] = jnp.where(i > 0, uprev_ref[0], 0.0)
    ext_ref[halo:halo + tm, :] = ua
    ext_ref[halo + tm:, :] = jnp.where(i < n_i - 1, unext_ref[0], 0.0)
    pos = i * tm + lax.broadcasted_iota(jnp.int32, (tm, 1), 0)
    mixed = []
    for g, w in enumerate(POOL_WINDOWS):
        cols = slice(g * POOL_GC, (g + 1) * POOL_GC)
        win = ext_ref[halo - w // 2:halo - w // 2 + tm, cols]
        for j in range(1, w):
            off = halo - w // 2 + j
            win = win + ext_ref[off:off + tm, cols]
        lo = jnp.maximum(pos - w // 2, 0)
        hi = jnp.minimum(pos + w // 2 - 1, seq_len - 1)
        cnt = (hi - lo + 1).astype(F32)
        pooled = win / cnt - ua[:, cols]
        mixed.append(_dot(pooled.astype(BF16), poolw_ref[g]))
    mixed = jnp.concatenate(mixed, axis=1) * pscale_ref[...]
    ya = _dot(mixed.astype(BF16), wa_ref[...])

    yb = _dot(yattn_ref[0], wb_ref[...])
    merged = sga_ref[0] * ya + sgb_ref[0] * yb
    h = _dot(merged.astype(BF16), wout_ref[...])
    g1 = mod_ref[0, 2:3, :]
    x1 = _ln_plain(DEEPNORM_ALPHA * x_ref[0] + g1 * h) * g_ref[...] + b_ref[...]
    x1_ref[0] = x1

    sh2 = mod_ref[0, 3:4, :]
    sc2 = mod_ref[0, 4:5, :]
    u2 = _ln_plain(x1) * (1.0 + sc2) + sh2
    logits = _dot_nt(rwt_ref[...], u2.astype(BF16)) + rb_ref[...]
    eio = lax.broadcasted_iota(jnp.int32, logits.shape, 0)
    vals, sels = [], []
    l = logits
    for _ in range(TOP_K):
        mv = jnp.max(l, axis=0, keepdims=True)
        idx = jnp.min(jnp.where(l == mv, eio, N_EXPERTS), axis=0, keepdims=True)
        sel = eio == idx
        vals.append(mv)
        sels.append(sel)
        l = jnp.where(sel, -jnp.inf, l)
    ex = [jnp.exp(v - vals[0]) for v in vals]
    den = ex[0] + ex[1] + ex[2] + ex[3]
    gate_ref[...] = jnp.concatenate([e / den for e in ex], axis=0)

    member = jnp.zeros(logits.shape, F32)
    for sel in sels:
        member = member + sel.astype(F32)
    member_b = member.astype(BF16)
    tok_before = (lax.broadcasted_iota(jnp.int32, (tm, tm), 0)
                  < lax.broadcasted_iota(jnp.int32, (tm, tm), 1)).astype(BF16)
    exp_before = (lax.broadcasted_iota(jnp.int32, (N_EXPERTS, N_EXPERTS), 1)
                  < lax.broadcasted_iota(jnp.int32, (N_EXPERTS, N_EXPERTS), 0)).astype(BF16)
    rank_in_tile = _dot(member_b, tok_before)
    seg_cnt = jnp.sum(member, axis=1, keepdims=True)
    seg_rows = jnp.ceil(seg_cnt * (1.0 / SEG_ALIGN)) * SEG_ALIGN
    seg_rows_b = jnp.broadcast_to(seg_rows, (N_EXPERTS, 128)).astype(BF16)
    seg_off = _dot(exp_before, seg_rows_b)[:, 0:1]
    base = seg_off + rank_in_tile
    pos = [jnp.sum(jnp.where(sel, base, 0.0), axis=0, keepdims=True) for sel in sels]
    pos_ref[...] = jnp.concatenate(pos, axis=0).astype(jnp.int32)
    run = run_ref[...]
    seg_ref[0] = jnp.concatenate([seg_rows, seg_off, run], axis=1).astype(jnp.int32)
    run_ref[...] = run + seg_rows
    cnt_ref[...] = jnp.broadcast_to(run + seg_rows, cnt_ref.shape)


def _mixer_call(x, ua, sga, sgb, yattn, mod, poolw, pscale, wa, wb, wout, g, b_, rwt, rb):
    b, s, d = x.shape
    tm = min(TOKEN_TILE, s)
    n_i = s // tm
    hb = tm // POOL_HALO
    n_hb = s // POOL_HALO
    row = lambda bi, i: (bi, i, 0)
    c2 = lambda bi, i: (0, 0)
    c3 = lambda bi, i: (0, 0, 0)
    tokcol = lambda bi, i: (0, bi * n_i + i)
    n_tok = b * s
    return pl.pallas_call(
        functools.partial(_mixer_kernel, seq_len=s),
        grid=(b, n_i),
        in_specs=[pl.BlockSpec((1, tm, d), row),
                  pl.BlockSpec((1, tm, d), row),
                  pl.BlockSpec((1, POOL_HALO, d), lambda bi, i: (bi, jnp.maximum(i * hb - 1, 0), 0)),
                  pl.BlockSpec((1, POOL_HALO, d), lambda bi, i: (bi, jnp.minimum((i + 1) * hb, n_hb - 1), 0)),
                  pl.BlockSpec((1, tm, d), row),
                  pl.BlockSpec((1, tm, d), row),
                  pl.BlockSpec((1, tm, d), row),
                  pl.BlockSpec((1, 6, d), lambda bi, i: (bi, 0, 0)),
                  pl.BlockSpec(poolw.shape, c3),
                  pl.BlockSpec((1, d), c2),
                  pl.BlockSpec((d, d), c2),
                  pl.BlockSpec((d, d), c2),
                  pl.BlockSpec((d, d), c2),
                  pl.BlockSpec((1, d), c2),
                  pl.BlockSpec((1, d), c2),
                  pl.BlockSpec((N_EXPERTS, d), c2),
                  pl.BlockSpec((N_EXPERTS, 1), c2)],
        out_specs=[pl.BlockSpec((1, tm, d), row),
                   pl.BlockSpec((TOP_K, tm), tokcol),
                   pl.BlockSpec((TOP_K, tm), tokcol),
                   pl.BlockSpec((1, N_EXPERTS, 3), lambda bi, i: (bi * n_i + i, 0, 0)),
                   pl.BlockSpec((N_EXPERTS, 128), c2)],
        out_shape=[jax.ShapeDtypeStruct((b, s, d), F32),
                   jax.ShapeDtypeStruct((TOP_K, n_tok), jnp.int32),
                   jax.ShapeDtypeStruct((TOP_K, n_tok), F32),
                   jax.ShapeDtypeStruct((b * n_i, N_EXPERTS, 3), jnp.int32),
                   jax.ShapeDtypeStruct((N_EXPERTS, 128), F32)],
        scratch_shapes=[pltpu.VMEM((tm + 2 * POOL_HALO, d), F32),
                        pltpu.VMEM((N_EXPERTS, 1), F32)],
        compiler_params=_params(2),
        name="mixer",
    )(x, ua, ua, ua, sga, sgb, yattn, mod, poolw, pscale, wa, wb, wout, g, b_, rwt, rb)


SEG_ALIGN = 8
SORTED_ROWS = TOP_K * MOE_TILE + SEG_ALIGN * N_EXPERTS
SEG_BITS = tuple(range(MOE_TILE.bit_length() - 1, SEG_ALIGN.bit_length() - 2, -1))


def _segment_copies(seg_ref, make_copy, action):
    def body(e, c):
        cnt = seg_ref[0, e, 0]
        src = seg_ref[0, e, 1]
        dst = seg_ref[0, e, 2]
        @pl.when(cnt > 0)
        def _():
            action(make_copy(pl.multiple_of(src, SEG_ALIGN), pl.multiple_of(dst, SEG_ALIGN),
                             pl.multiple_of(cnt, SEG_ALIGN)))
        return c

    lax.fori_loop(0, N_EXPERTS, body, 0)


def _sort_matrix(pos, n_sorted, transposed):
    if transposed:
        grid = lax.broadcasted_iota(jnp.int32, (pos.shape[0], n_sorted), 1)
        hit = grid == pos[:, 0:1]
        for k in range(1, TOP_K):
            hit = hit | (grid == pos[:, k:k + 1])
    else:
        grid = lax.broadcasted_iota(jnp.int32, (n_sorted, pos.shape[1]), 0)
        hit = grid == pos[0:1, :]
        for k in range(1, TOP_K):
            hit = hit | (grid == pos[k:k + 1, :])
    return jnp.where(hit, 1.0, 0.0).astype(BF16)


def _dispatch_kernel(pends_ref, padded_ref, n_used_ref,
                     seg_ref, seg_prev_ref, x1_ref, pos_ref, mod_ref, xb_ref,
                     sorted_ref, zero_ref, sem, seg_sems):
    td, d = x1_ref.shape
    n_blocks = xb_ref.shape[0] // EXPERT_ROWS

    @pl.when(pl.program_id(0) == 0)
    def _():
        zero_ref[...] = jnp.zeros_like(zero_ref)

        def zero_copy(row0):
            return pltpu.make_async_copy(
                zero_ref, xb_ref.at[pl.ds(pl.multiple_of(row0, EXPERT_ROWS), EXPERT_ROWS)], sem)

        def zero_blocks(action):
            def per_expert(e, c):
                @pl.when(padded_ref[e] > 0)
                def _():
                    action(zero_copy(pends_ref[e] - EXPERT_ROWS))
                return c

            def per_unused(r, c):
                action(zero_copy(r * EXPERT_ROWS))
                return c

            lax.fori_loop(0, N_EXPERTS, per_expert, 0)
            lax.fori_loop(n_used_ref[0], n_blocks, per_unused, 0)

        zero_blocks(lambda cp: cp.start())
        zero_blocks(lambda cp: cp.wait())

    sh2 = mod_ref[0, 3:4, :]
    sc2 = mod_ref[0, 4:5, :]
    u = (_ln_plain(x1_ref[...]) * (1.0 + sc2) + sh2).astype(BF16)
    rows = _dot(_sort_matrix(pos_ref[...], SORTED_ROWS, transposed=False), u)
    lo = lax.bitcast_convert_type(rows[:, :d // 2], jnp.uint32)
    hi = lax.bitcast_convert_type(rows[:, d // 2:], jnp.uint32)
    step = pl.program_id(0)
    slot = step % 2
    sorted_ref[slot] = hi | (lo >> 16)

    def copies_from(s):
        def make_copy(src, dst, size):
            return pltpu.make_async_copy(sorted_ref.at[s, pl.ds(src, size)],
                                         xb_ref.at[pl.ds(dst, size)], seg_sems.at[s])
        return make_copy

    _segment_copies(seg_ref, copies_from(slot), lambda cp: cp.start())

    @pl.when(step > 0)
    def _():
        _segment_copies(seg_prev_ref, copies_from(1 - slot), lambda cp: cp.wait())

    @pl.when(step == pl.num_programs(0) - 1)
    def _():
        _segment_copies(seg_ref, copies_from(slot), lambda cp: cp.wait())


def _dispatch_call(pends, padded, n_used, seg, x1_flat, pos, mod, n_rows, seq_len):
    n_tok, d = x1_flat.shape
    td = MOE_TILE
    return pl.pallas_call(
        _dispatch_kernel,
        grid_spec=pltpu.PrefetchScalarGridSpec(
            num_scalar_prefetch=3,
            grid=(n_tok // td,),
            in_specs=[pl.BlockSpec((1, N_EXPERTS, 3), lambda i, *_: (i, 0, 0), memory_space=pltpu.SMEM),
                      pl.BlockSpec((1, N_EXPERTS, 3), lambda i, *_: (jnp.maximum(i - 1, 0), 0, 0),
                                   memory_space=pltpu.SMEM),
                      pl.BlockSpec((td, d), lambda i, *_: (i, 0)),
                      pl.BlockSpec((TOP_K, td), lambda i, *_: (0, i)),
                      pl.BlockSpec((1, 6, d), lambda i, *_: ((i * td) // seq_len, 0, 0))],
            out_specs=pl.BlockSpec(memory_space=pl.ANY),
            scratch_shapes=[pltpu.VMEM((2, SORTED_ROWS, d // 2), jnp.uint32),
                            pltpu.VMEM((EXPERT_ROWS, d // 2), jnp.uint32),
                            pltpu.SemaphoreType.DMA,
                            pltpu.SemaphoreType.DMA((2,))]),
        out_shape=jax.ShapeDtypeStruct((n_rows, d // 2), jnp.uint32),
        compiler_params=_params(1, has_side_effects=True),
        name="dispatch",
    )(pends, padded, n_used, seg, seg, x1_flat, pos, mod)


CAST_ROWS = 128


def _expert_kernel(blk_e_ref, n_used_ref, first_ref, slot_ref, next_ref,
                   xb_ref, bup_ref, bdn_ref, wup_hbm, wdn_hbm, yb_ref,
                   wup32_ref, wdn32_ref, wup_ref, wdn_ref, sems):
    r = pl.program_id(0)
    used = r < n_used_ref[0]

    def weight_copies(e, slot):
        return (pltpu.make_async_copy(wup_hbm.at[e], wup32_ref.at[slot], sems.at[slot, 0]),
                pltpu.make_async_copy(wdn_hbm.at[e], wdn32_ref.at[slot], sems.at[slot, 1]))

    @pl.when(jnp.logical_not(used))
    def _():
        yb_ref[...] = jnp.zeros_like(yb_ref)

    @pl.when(used & (first_ref[r] == 1))
    def _():
        slot = slot_ref[r]

        @pl.when(r == 0)
        def _():
            for cp in weight_copies(blk_e_ref[r], slot):
                cp.start()

        for cp in weight_copies(blk_e_ref[r], slot):
            cp.wait()
        nxt = next_ref[r]

        @pl.when(nxt >= 0)
        def _():
            for cp in weight_copies(nxt, 1 - slot):
                cp.start()

        def cast(i, c):
            rows_ = pl.ds(pl.multiple_of(i * CAST_ROWS, CAST_ROWS), CAST_ROWS)
            wup_ref[rows_, :] = wup32_ref[slot, rows_, :].astype(BF16)
            wdn_ref[rows_, :] = wdn32_ref[slot, rows_, :].astype(BF16)
            return c

        lax.fori_loop(0, D_MODEL // CAST_ROWS, cast, 0)

    @pl.when(used)
    def _():
        words = xb_ref[...]
        lo = lax.bitcast_convert_type(words << 16, F32).astype(BF16)
        hi = lax.bitcast_convert_type(words & jnp.uint32(0xFFFF0000), F32).astype(BF16)
        half = D_MODEL // 2
        h = _dot(lo, wup_ref[:half, :]) + _dot(hi, wup_ref[half:, :]) + bup_ref[0]
        gate = jnp.minimum(h[:, :D_FF], SWIGLU_LIMIT)
        up = jnp.clip(h[:, D_FF:], -SWIGLU_LIMIT, SWIGLU_LIMIT)
        act = (up + 1.0) * (gate * jax.nn.sigmoid(SWIGLU_ALPHA * gate))
        yb_ref[...] = _dot(act.astype(BF16), wdn_ref[...]) + bdn_ref[0]


def _expert_call(blk_e, n_used, blk_first, blk_slot, blk_next, xb, wup, bup, wdn, bdn):
    n_rows = xb.shape[0]
    d = D_MODEL
    assert D_FF == d
    rows = EXPERT_ROWS
    last = lambda r, be, nu, *_: jnp.maximum(jnp.minimum(r, nu[0] - 1), 0)
    exp3 = lambda r, be, nu, *_: (be[last(r, be, nu)], 0, 0)
    return pl.pallas_call(
        _expert_kernel,
        grid_spec=pltpu.PrefetchScalarGridSpec(
            num_scalar_prefetch=5,
            grid=(n_rows // rows,),
            in_specs=[pl.BlockSpec((rows, d // 2), lambda r, be, nu, *_: (last(r, be, nu), 0)),
                      pl.BlockSpec((1, 1, 2 * D_FF), exp3),
                      pl.BlockSpec((1, 1, d), exp3),
                      pl.BlockSpec(memory_space=pl.ANY),
                      pl.BlockSpec(memory_space=pl.ANY)],
            out_specs=pl.BlockSpec((rows, d), lambda r, *_: (r, 0)),
            scratch_shapes=[pltpu.VMEM((2, d, 2 * D_FF), F32),
                            pltpu.VMEM((2, D_FF, d), F32),
                            pltpu.VMEM((d, 2 * D_FF), BF16),
                            pltpu.VMEM((D_FF, d), BF16),
                            pltpu.SemaphoreType.DMA((2, 2))]),
        out_shape=jax.ShapeDtypeStruct((n_rows, d), F32),
        compiler_params=_params(1),
        name="experts",
    )(blk_e, n_used, blk_first, blk_slot, blk_next, xb, bup, bdn, wup, wdn)


def _combine_kernel(seg_ref, seg_next_ref, x1_ref, pos_ref, post_ref, gates_ref, mod_ref, g_ref, b_ref,
                    yb_ref, o_ref, ybuf_ref, sems):
    tc = x1_ref.shape[0]
    n_sorted = SORTED_ROWS
    step = pl.program_id(0)
    slot = step % 2

    def copies_into(s):
        def make_copy(src, dst, size):
            return pltpu.make_async_copy(yb_ref.at[pl.ds(dst, size)],
                                         ybuf_ref.at[s, pl.ds(src, size)], sems.at[s])
        return make_copy

    def fetch(tile_seg_ref, s):
        ybuf_ref[s, TOP_K * tc:, :] = jnp.zeros((n_sorted - TOP_K * tc, ybuf_ref.shape[2]), F32)
        _segment_copies(tile_seg_ref, copies_into(s), lambda cp: cp.start())

    @pl.when(step == 0)
    def _():
        fetch(seg_ref, slot)

    @pl.when(step + 1 < pl.num_programs(0))
    def _():
        fetch(seg_next_ref, 1 - slot)

    pos = pos_ref[...]
    gates = gates_ref[...]
    grid = lax.broadcasted_iota(jnp.int32, (n_sorted, tc), 0)
    gsel = jnp.where(grid == pos[0:1, :], gates[0:1, :], 0.0)
    for k in range(1, TOP_K):
        gsel = gsel + jnp.where(grid == pos[k:k + 1, :], gates[k:k + 1, :], 0.0)
    gate_col = jnp.sum(gsel, axis=1, keepdims=True)
    unsort = _sort_matrix(post_ref[...], n_sorted, transposed=True)

    _segment_copies(seg_ref, copies_into(slot), lambda cp: cp.wait())

    yg = ybuf_ref[slot] * gate_col
    y_hi = yg.astype(BF16)
    y_lo = (yg - y_hi.astype(F32)).astype(BF16)
    h2 = _dot(unsort, y_hi) + _dot(unsort, y_lo)
    g2 = mod_ref[0, 5:6, :]
    o_ref[...] = _ln_plain(DEEPNORM_ALPHA * x1_ref[...] + g2 * h2) * g_ref[...] + b_ref[...]


def _combine_call(seg, x1_flat, pos, pos_t, gates, mod, g, b_, yb, seq_len):
    n_tok, d = x1_flat.shape
    tc = MOE_TILE
    c2 = lambda i: (0, 0)
    n_tiles = n_tok // tc
    return pl.pallas_call(
        _combine_kernel,
        grid=(n_tiles,),
        in_specs=[pl.BlockSpec((1, N_EXPERTS, 3), lambda i: (i, 0, 0), memory_space=pltpu.SMEM),
                  pl.BlockSpec((1, N_EXPERTS, 3), lambda i: (jnp.minimum(i + 1, n_tiles - 1), 0, 0),
                               memory_space=pltpu.SMEM),
                  pl.BlockSpec((tc, d), lambda i: (i, 0)),
                  pl.BlockSpec((TOP_K, tc), lambda i: (0, i)),
                  pl.BlockSpec((tc, TOP_K), lambda i: (i, 0)),
                  pl.BlockSpec((TOP_K, tc), lambda i: (0, i)),
                  pl.BlockSpec((1, 6, d), lambda i: ((i * tc) // seq_len, 0, 0)),
                  pl.BlockSpec((1, d), c2),
                  pl.BlockSpec((1, d), c2),
                  pl.BlockSpec(memory_space=pl.ANY)],
        out_specs=pl.BlockSpec((tc, d), lambda i: (i, 0)),
        out_shape=jax.ShapeDtypeStruct((n_tok, d), F32),
        scratch_shapes=[pltpu.VMEM((2, SORTED_ROWS, d), F32), pltpu.SemaphoreType.DMA((2,))],
        compiler_params=_params(1),
        name="combine",
    )(seg, seg, x1_flat, pos, pos_t, gates, mod, g, b_, yb)


def _rope_tables(seq_len):
    inv_freq = ROPE_THETA ** (-jnp.arange(0, HEAD_DIM, 2, dtype=F32) / HEAD_DIM)
    ang = jnp.arange(seq_len, dtype=F32)[:, None] * inv_freq[None, :]
    cos, sin = lax.optimization_barrier((jnp.cos(ang), jnp.sin(ang)))
    zero = jnp.zeros_like(sin)
    reps = HEAD_W // HEAD_DIM
    cos_t = jnp.tile(jnp.concatenate([cos, cos], axis=1), (1, reps))
    sina_t = jnp.tile(jnp.concatenate([-sin, zero], axis=1), (1, reps))
    sinb_t = jnp.tile(jnp.concatenate([zero, sin], axis=1), (1, reps))
    return cos_t, sina_t, sinb_t


def _encode(x, mod, w):
    b, s, d = x.shape
    n_tok = b * s
    cos, sina, sinb = w["rope"]
    ua, q, k, vt, sga, sgb = _inproj_call(x, mod, cos, sina, sinb, w["wcat"], w["wvt"])
    yattn = _attn_call(q, k, vt, w["lq1"], w["lk1"], w["lq2"], w["lk2"], w["subw"])
    x1, pos, gates, seg, cnt = _mixer_call(
        x, ua, sga, sgb, yattn, mod, w["poolw"], w["pscale"], w["wa"], w["wb"], w["wout"],
        w["ln1_g"], w["ln1_b"], w["rwt"], w["rb"])

    rows = EXPERT_ROWS
    counts = cnt[:, 0].astype(jnp.int32)
    padded = (counts + rows - 1) // rows * rows
    pends = jnp.cumsum(padded)
    pstarts = pends - padded
    seg = seg.at[:, :, 2].add(pstarts[None, :])
    n_tiles = n_tok // MOE_TILE
    n_blocks = (n_tiles * SORTED_ROWS) // rows + N_EXPERTS
    blk_row0 = jnp.arange(n_blocks, dtype=jnp.int32) * rows
    blk_e = jnp.minimum(jnp.sum((pends[None, :] <= blk_row0[:, None]).astype(jnp.int32), axis=1),
                        N_EXPERTS - 1)
    n_used = (pends[-1:] // rows).astype(jnp.int32)
    blk_ids = jnp.arange(n_blocks, dtype=jnp.int32)
    blk_first = ((blk_ids == 0) | (blk_e != jnp.roll(blk_e, 1))).astype(jnp.int32)
    blk_slot = (jnp.cumsum(blk_first) - 1) % 2
    e_ids = jnp.arange(N_EXPERTS, dtype=jnp.int32)
    later = (e_ids[None, :] > e_ids[:, None]) & (padded[None, :] > 0)
    next_present = jnp.min(jnp.where(later, e_ids[None, :], N_EXPERTS), axis=1)
    next_present = jnp.where(next_present == N_EXPERTS, -1, next_present)
    blk_next = jnp.sum(jnp.where(blk_e[:, None] == e_ids[None, :], next_present[None, :], 0), axis=1)

    x1_flat = x1.reshape(n_tok, d)
    xb = _dispatch_call(pends.astype(jnp.int32), padded, n_used, seg, x1_flat, pos, mod, n_blocks * rows, s)
    yb = _expert_call(blk_e, n_used, blk_first, blk_slot.astype(jnp.int32), blk_next.astype(jnp.int32),
                      xb, w["wup"], w["bup"], w["wdn"], w["bdn"])
    out = _combine_call(seg, x1_flat, pos, pos.T, gates, mod, w["ln2_g"], w["ln2_b"], yb, s)
    return out.reshape(b, s, d)


def kernel(x_prompt, x_sample, c_prompt, c_sample, w_ada, b_ada, w_in, pool_w, pool_scale, w_branch_a, w_branch_b, lambda_q1, lambda_k1, lambda_q2, lambda_k2, subln_w, w_out, ln1_g, ln1_b, router_w, router_b, exp_w_up, exp_b_up, exp_w_down, exp_b_down, ln2_g, ln2_b):
    d = D_MODEL
    w_in0 = w_in[0]
    seg = lambda j: w_in0[:, j * d:(j + 1) * d]
    w = {
        "wcat": jnp.concatenate([seg(0), seg(1), seg(2), seg(4), seg(5)], axis=1).astype(BF16),
        "wvt": seg(3).T.astype(BF16),
        "lq1": lambda_q1, "lk1": lambda_k1, "lq2": lambda_q2, "lk2": lambda_k2,
        "subw": subln_w[0].reshape(HEAD_W, 1),
        "poolw": pool_w[0].astype(BF16),
        "pscale": pool_scale,
        "wa": w_branch_a[0].astype(BF16),
        "wb": w_branch_b[0].astype(BF16),
        "wout": w_out[0].astype(BF16),
        "ln1_g": ln1_g, "ln1_b": ln1_b,
        "rwt": router_w[0].T.astype(BF16),
        "rb": router_b[0].reshape(N_EXPERTS, 1),
        "wup": exp_w_up[0],
        "bup": exp_b_up[0].reshape(N_EXPERTS, 1, 2 * D_FF),
        "wdn": exp_w_down[0],
        "bdn": exp_b_down[0].reshape(N_EXPERTS, 1, d),
        "ln2_g": ln2_g, "ln2_b": ln2_b,
        "rope": _rope_tables(max(x_prompt.shape[1], x_sample.shape[1])),
    }
    nb_p, nb_s = c_prompt.shape[0], c_sample.shape[0]
    pad = (-(nb_p + nb_s)) % 8
    c_all = jnp.concatenate([c_prompt, c_sample, jnp.zeros((pad, d), F32)], axis=0)
    mod = _mod_call(c_all, w_ada[0], b_ada[0]).reshape(-1, 6, d)
    y_prompt = _encode(x_prompt, mod[:nb_p], w)
    y_sample = _encode(x_sample, mod[nb_p:nb_p + nb_s], w)
    return (y_prompt, y_sample)
```

```python
import functools

import jax
import jax.numpy as jnp
from jax import lax
from jax.experimental import pallas as pl
from jax.experimental.pallas import tpu as pltpu

F32 = jnp.float32
BF16 = jnp.bfloat16

D_MODEL = 1024
N_HEADS = 8
HEAD_DIM = 64
HEAD_W = 2 * HEAD_DIM
ROPE_THETA = 10000.0
POOL_WINDOWS = (2, 4, 8, 16)
POOL_GC = 256
POOL_HALO = 8
N_EXPERTS = 32
TOP_K = 4
D_FF = 1024
SWIGLU_LIMIT = 7.0
SWIGLU_ALPHA = 1.702
LN_EPS = 1e-5
DEPTH = 1
DEEPNORM_ALPHA = (2.0 * DEPTH) ** 0.25
LAMBDA_INIT = 0.8 - 0.6 * 1.0

VMEM_LIMIT_V7X = 56 * 1024 * 1024

TOKEN_TILE = 256
MOE_TILE = TOKEN_TILE
ATTN_Q_TILE = 1024
ATTN_K_CHUNK = 1024
LOG2_E = 1.4426950408889634
EXPERT_ROWS = 256


def _params(n_axes, **kw):
    return pltpu.CompilerParams(dimension_semantics=("arbitrary",) * n_axes,
                                vmem_limit_bytes=VMEM_LIMIT_V7X, **kw)


def _ln_plain(x):
    mu = jnp.mean(x, axis=-1, keepdims=True)
    xc = x - mu
    var = jnp.mean(xc * xc, axis=-1, keepdims=True)
    return xc * lax.rsqrt(var + LN_EPS)


def _dot(a, b):
    return jnp.dot(a, b, preferred_element_type=F32)


def _dot_nt(a, b, **kw):
    return lax.dot_general(a, b, (((1,), (1,)), ((), ())), preferred_element_type=F32, **kw)


def _mod_kernel(c_ref, w_ref, b_ref, o_ref):
    c = c_ref[...]
    s = c * jax.nn.sigmoid(c)
    o_ref[...] = jnp.dot(s, w_ref[...], preferred_element_type=F32,
                         precision=lax.Precision.HIGHEST) + b_ref[...]


def _mod_call(c, w_ada, b_ada):
    rows, d = c.shape
    n_out = w_ada.shape[1]
    return pl.pallas_call(
        _mod_kernel,
        grid=(n_out // d,),
        in_specs=[pl.BlockSpec((rows, d), lambda j: (0, 0)),
                  pl.BlockSpec((d, d), lambda j: (0, j)),
                  pl.BlockSpec((1, d), lambda j: (0, j))],
        out_specs=pl.BlockSpec((rows, d), lambda j: (0, j)),
        out_shape=jax.ShapeDtypeStruct((rows, n_out), F32),
        compiler_params=_params(1),
        name="mod",
    )(c, w_ada, b_ada.reshape(1, n_out))


def _inproj_kernel(x_ref, mod_ref, cos_ref, sina_ref, sinb_ref, wcat_ref, wvt_ref,
                   ua_ref, q_ref, k_ref, vt_ref, sga_ref, sgb_ref):
    d = D_MODEL
    x = x_ref[0]
    sh1 = mod_ref[0, 0:1, :]
    sc1 = mod_ref[0, 1:2, :]
    u = (_ln_plain(x) * (1.0 + sc1) + sh1).astype(BF16)

    def proj(j):
        return _dot(u, wcat_ref[:, j * d:(j + 1) * d])

    reps = d // HEAD_W
    cos = jnp.concatenate([cos_ref[...]] * reps, axis=1)
    sina = jnp.concatenate([sina_ref[...]] * reps, axis=1)
    sinb = jnp.concatenate([sinb_ref[...]] * reps, axis=1)

    def rope(t):
        half = HEAD_DIM // 2
        return t * cos + pltpu.roll(t, d - half, 1) * sina + pltpu.roll(t, half, 1) * sinb

    ua_ref[0] = proj(0)
    q_ref[0] = (rope(proj(1)) * (HEAD_DIM ** -0.5 * LOG2_E)).astype(BF16)
    k_ref[0] = rope(proj(2)).astype(BF16)
    vt_ref[0] = _dot_nt(wvt_ref[...], u).astype(BF16)
    sga_ref[0] = jax.nn.sigmoid(proj(3))
    sgb_ref[0] = jax.nn.sigmoid(proj(4))


def _inproj_call(x, mod, cos, sina, sinb, wcat, wvt):
    b, s, d = x.shape
    tm = min(TOKEN_TILE, s)
    row = lambda bi, i: (bi, i, 0)
    const2 = lambda bi, i: (0, 0)
    tab = pl.BlockSpec((tm, HEAD_W), lambda bi, i: (i, 0))
    return pl.pallas_call(
        _inproj_kernel,
        grid=(b, s // tm),
        in_specs=[pl.BlockSpec((1, tm, d), row),
                  pl.BlockSpec((1, 6, d), lambda bi, i: (bi, 0, 0)),
                  tab, tab, tab,
                  pl.BlockSpec(wcat.shape, const2),
                  pl.BlockSpec(wvt.shape, const2)],
        out_specs=[pl.BlockSpec((1, tm, d), row),
                   pl.BlockSpec((1, tm, d), row),
                   pl.BlockSpec((1, tm, d), row),
                   pl.BlockSpec((1, d, tm), lambda bi, i: (bi, 0, i)),
                   pl.BlockSpec((1, tm, d), row),
                   pl.BlockSpec((1, tm, d), row)],
        out_shape=[jax.ShapeDtypeStruct((b, s, d), F32),
                   jax.ShapeDtypeStruct((b, s, d), BF16),
                   jax.ShapeDtypeStruct((b, s, d), BF16),
                   jax.ShapeDtypeStruct((b, d, s), BF16),
                   jax.ShapeDtypeStruct((b, s, d), F32),
                   jax.ShapeDtypeStruct((b, s, d), F32)],
        compiler_params=_params(2),
        name="inproj",
    )(x, mod, cos, sina, sinb, wcat, wvt)


def _attn_kernel(q_ref, k_ref, vt_ref, lq1_ref, lk1_ref, lq2_ref, lk2_ref, subw_ref,
                 o_ref, qt_ref, sa_ref, sb_ref, o1_ref, o2_ref, *, k_chunk):
    tq = q_ref.shape[1]
    s_len = k_ref.shape[1]
    q = q_ref[0]
    lane = lax.broadcasted_iota(jnp.int32, q.shape, 1)
    zero = jnp.zeros_like(q)
    qt_ref[0] = jnp.where(lane < HEAD_DIM, q, zero).T
    qt_ref[1] = jnp.where(lane >= HEAD_DIM, q, zero).T
    o1_ref[...] = jnp.zeros_like(o1_ref)
    o2_ref[...] = jnp.zeros_like(o2_ref)
    n_chunks = s_len // k_chunk

    def scores(c, s_ref):
        start = pl.multiple_of(c * k_chunk, k_chunk)
        kc = k_ref[0, pl.ds(start, k_chunk), :]
        s1 = _dot(kc, qt_ref[0])
        s2 = _dot(kc, qt_ref[1])
        s_ref[0] = s1
        s_ref[1] = s2
        return jnp.max(s1, axis=0, keepdims=True), jnp.max(s2, axis=0, keepdims=True)

    def accumulate(c, s_ref, cmax, state):
        start = pl.multiple_of(c * k_chunk, k_chunk)
        vtc = vt_ref[0, :, pl.ds(start, k_chunk)]
        out = []
        for j, acc_ref in enumerate((o1_ref, o2_ref)):
            m, l = state[j]
            m_new = jnp.maximum(m, cmax[j])
            alpha = jnp.exp2(m - m_new)
            p = jnp.exp2(s_ref[j] - m_new)
            l_new = alpha * l + jnp.sum(p, axis=0, keepdims=True)
            acc_ref[...] = alpha * acc_ref[...] + _dot(vtc, p.astype(BF16))
            out.append((m_new, l_new))
        return tuple(out)

    def body(i, carry):
        cmax_a, m = carry
        cmax_b = scores(2 * i + 1, sb_ref)
        m = accumulate(2 * i, sa_ref, cmax_a, m)
        cmax_a = scores(2 * i + 2, sa_ref)
        m = accumulate(2 * i + 1, sb_ref, cmax_b, m)
        return cmax_a, m

    init = (jnp.full((1, tq), -jnp.inf, F32), jnp.zeros((1, tq), F32))
    cmax_a, m = lax.fori_loop(0, n_chunks // 2 - 1, body, (scores(0, sa_ref), (init, init)))
    cmax_b = scores(n_chunks - 1, sb_ref)
    m = accumulate(n_chunks - 2, sa_ref, cmax_a, m)
    (_, l1), (_, l2) = accumulate(n_chunks - 1, sb_ref, cmax_b, m)

    lam = (jnp.exp(jnp.sum(lq1_ref[...] * lk1_ref[...], axis=1, keepdims=True))
           - jnp.exp(jnp.sum(lq2_ref[...] * lk2_ref[...], axis=1, keepdims=True))
           + LAMBDA_INIT)
    o = o1_ref[...] / l1 - lam * (o2_ref[...] / l2)
    o = o * lax.rsqrt(jnp.mean(o * o, axis=0, keepdims=True) + LN_EPS)
    o = o * subw_ref[...] * (1.0 - LAMBDA_INIT)
    o_ref[0] = o.T.astype(o_ref.dtype)


def _attn_call(q, k, vt, lq1, lk1, lq2, lk2, subw):
    b, s, d = q.shape
    tq = min(ATTN_Q_TILE, s)
    kc = min(ATTN_K_CHUNK, s // 4)
    assert s % (2 * kc) == 0 and s % tq == 0
    lam_spec = pl.BlockSpec((1, HEAD_DIM), lambda bi, h, i: (0, 0))
    return pl.pallas_call(
        functools.partial(_attn_kernel, k_chunk=kc),
        grid=(b, N_HEADS, s // tq),
        in_specs=[pl.BlockSpec((1, tq, HEAD_W), lambda bi, h, i: (bi, i, h)),
                  pl.BlockSpec((1, s, HEAD_W), lambda bi, h, i: (bi, 0, h)),
                  pl.BlockSpec((1, HEAD_W, s), lambda bi, h, i: (bi, h, 0)),
                  lam_spec, lam_spec, lam_spec, lam_spec,
                  pl.BlockSpec((HEAD_W, 1), lambda bi, h, i: (0, 0))],
        out_specs=pl.BlockSpec((1, tq, HEAD_W), lambda bi, h, i: (bi, i, h)),
        out_shape=jax.ShapeDtypeStruct((b, s, d), BF16),
        scratch_shapes=[pltpu.VMEM((2, HEAD_W, tq), BF16),
                        pltpu.VMEM((2, kc, tq), F32), pltpu.VMEM((2, kc, tq), F32),
                        pltpu.VMEM((HEAD_W, tq), F32),
                        pltpu.VMEM((HEAD_W, tq), F32)],
        compiler_params=_params(3),
        name="attn",
    )(q, k, vt, lq1, lk1, lq2, lk2, subw)


def _mixer_kernel(x_ref, ua_ref, uprev_ref, unext_ref, sga_ref, sgb_ref, yattn_ref, mod_ref,
                  poolw_ref, pscale_ref, wa_ref, wb_ref, wout_ref, g_ref, b_ref, rwt_ref, rb_ref,
                  x1_ref, pos_ref, gate_ref, seg_ref, cnt_ref,
                  ext_ref, part_ref, run_ref, *, seq_len):
    tm = x_ref.shape[1]
    halo = POOL_HALO
    bi = pl.program_id(0)
    i = pl.program_id(1)
    n_i = pl.num_programs(1)

    @pl.when((bi == 0) & (i == 0))
    def _():
        run_ref[...] = jnp.zeros_like(run_ref)

    ua = ua_ref[0]
    ext_ref[0:halo, :] = jnp.where(i > 0, uprev_ref[0], 0.0)
    ext_ref[halo:halo + tm, :] = ua
    ext_ref[halo + tm:, :] = jnp.where(i < n_i - 1, unext_ref[0], 0.0)
    pos = i * tm + lax.broadcasted_iota(jnp.int32, (tm, 1), 0)
    mixed = []
    for g, w in enumerate(POOL_WINDOWS):
        cols = slice(g * POOL_GC, (g + 1) * POOL_GC)
        src, src_cols, span, rows = ext_ref, cols, 1, tm + 2 * halo
        level = 0
        while 2 * span < w:
            rows -= span
            part_ref[level, 0:rows, :] = src[0:rows, src_cols] + src[span:span + rows, src_cols]
            src, src_cols, span = part_ref.at[level], slice(None), 2 * span
            level += 1
        first = halo - w // 2
        win = src[first:first + tm, src_cols] + src[first + span:first + span + tm, src_cols]
        lo = jnp.maximum(pos - w // 2, 0)
        hi = jnp.minimum(pos + w // 2 - 1, seq_len - 1)
        cnt = (hi - lo + 1).astype(F32)
        pooled = win / cnt - ua[:, cols]
        mixed.append(_dot(pooled.astype(BF16), poolw_ref[g]))
    mixed = jnp.concatenate(mixed, axis=1) * pscale_ref[...]
    ya = _dot(mixed.astype(BF16), wa_ref[...])

    yb = _dot(yattn_ref[0], wb_ref[...])
    merged = sga_ref[0] * ya + sgb_ref[0] * yb
    h = _dot(merged.astype(BF16), wout_ref[...])
    g1 = mod_ref[0, 2:3, :]
    x1 = _ln_plain(DEEPNORM_ALPHA * x_ref[0] + g1 * h) * g_ref[...] + b_ref[...]
    x1_ref[0] = x1

    sh2 = mod_ref[0, 3:4, :]
    sc2 = mod_ref[0, 4:5, :]
    u2 = _ln_plain(x1) * (1.0 + sc2) + sh2
    logits = _dot_nt(rwt_ref[...], u2.astype(BF16)) + rb_ref[...]
    eio = lax.broadcasted_iota(jnp.int32, logits.shape, 0)
    vals, sels = [], []
    l = logits
    for _ in range(TOP_K):
        mv = jnp.max(l, axis=0, keepdims=True)
        idx = jnp.min(jnp.where(l == mv, eio, N_EXPERTS), axis=0, keepdims=True)
        sel = eio == idx
        vals.append(mv)
        sels.append(sel)
        l = jnp.where(sel, -jnp.inf, l)
    ex = [jnp.exp(v - vals[0]) for v in vals]
    den = ex[0] + ex[1] + ex[2] + ex[3]
    gate_ref[...] = jnp.concatenate([e / den for e in ex], axis=0)

    member = jnp.zeros(logits.shape, F32)
    for sel in sels:
        member = member + sel.astype(F32)
    member_b = member.astype(BF16)
    tok_before = (lax.broadcasted_iota(jnp.int32, (tm, tm), 0)
                  < lax.broadcasted_iota(jnp.int32, (tm, tm), 1)).astype(BF16)
    exp_before = (lax.broadcasted_iota(jnp.int32, (N_EXPERTS, N_EXPERTS), 1)
                  < lax.broadcasted_iota(jnp.int32, (N_EXPERTS, N_EXPERTS), 0)).astype(BF16)
    rank_in_tile = _dot(member_b, tok_before)
    seg_cnt = jnp.sum(member, axis=1, keepdims=True)
    seg_rows = jnp.ceil(seg_cnt * (1.0 / SEG_ALIGN)) * SEG_ALIGN
    seg_rows_b = jnp.broadcast_to(seg_rows, (N_EXPERTS, 128)).astype(BF16)
    seg_off = _dot(exp_before, seg_rows_b)[:, 0:1]
    base = seg_off + rank_in_tile
    pos = [jnp.sum(jnp.where(sel, base, 0.0), axis=0, keepdims=True) for sel in sels]
    pos_ref[...] = jnp.concatenate(pos, axis=0).astype(jnp.int32)
    run = run_ref[...]
    seg_ref[0] = jnp.concatenate([seg_rows, seg_off, run], axis=1).astype(jnp.int32)
    run_ref[...] = run + seg_rows
    cnt_ref[...] = jnp.broadcast_to(run + seg_rows, cnt_ref.shape)


def _mixer_call(x, ua, sga, sgb, yattn, mod, poolw, pscale, wa, wb, wout, g, b_, rwt, rb):
    b, s, d = x.shape
    tm = min(TOKEN_TILE, s)
    n_i = s // tm
    hb = tm // POOL_HALO
    n_hb = s // POOL_HALO
    row = lambda bi, i: (bi, i, 0)
    c2 = lambda bi, i: (0, 0)
    c3 = lambda bi, i: (0, 0, 0)
    tokcol = lambda bi, i: (0, bi * n_i + i)
    n_tok = b * s
    return pl.pallas_call(
        functools.partial(_mixer_kernel, seq_len=s),
        grid=(b, n_i),
        in_specs=[pl.BlockSpec((1, tm, d), row),
                  pl.BlockSpec((1, tm, d), row),
                  pl.BlockSpec((1, POOL_HALO, d), lambda bi, i: (bi, jnp.maximum(i * hb - 1, 0), 0)),
                  pl.BlockSpec((1, POOL_HALO, d), lambda bi, i: (bi, jnp.minimum((i + 1) * hb, n_hb - 1), 0)),
                  pl.BlockSpec((1, tm, d), row),
                  pl.BlockSpec((1, tm, d), row),
                  pl.BlockSpec((1, tm, d), row),
                  pl.BlockSpec((1, 6, d), lambda bi, i: (bi, 0, 0)),
                  pl.BlockSpec(poolw.shape, c3),
                  pl.BlockSpec((1, d), c2),
                  pl.BlockSpec((d, d), c2),
                  pl.BlockSpec((d, d), c2),
                  pl.BlockSpec((d, d), c2),
                  pl.BlockSpec((1, d), c2),
                  pl.BlockSpec((1, d), c2),
                  pl.BlockSpec((N_EXPERTS, d), c2),
                  pl.BlockSpec((N_EXPERTS, 1), c2)],
        out_specs=[pl.BlockSpec((1, tm, d), row),
                   pl.BlockSpec((TOP_K, tm), tokcol),
                   pl.BlockSpec((TOP_K, tm), tokcol),
                   pl.BlockSpec((1, N_EXPERTS, 3), lambda bi, i: (bi * n_i + i, 0, 0)),
                   pl.BlockSpec((N_EXPERTS, 128), c2)],
        out_shape=[jax.ShapeDtypeStruct((b, s, d), F32),
                   jax.ShapeDtypeStruct((TOP_K, n_tok), jnp.int32),
                   jax.ShapeDtypeStruct((TOP_K, n_tok), F32),
                   jax.ShapeDtypeStruct((b * n_i, N_EXPERTS, 3), jnp.int32),
                   jax.ShapeDtypeStruct((N_EXPERTS, 128), F32)],
        scratch_shapes=[pltpu.VMEM((tm + 2 * POOL_HALO, d), F32),
                        pltpu.VMEM((max(POOL_WINDOWS).bit_length() - 2, tm + 2 * POOL_HALO, POOL_GC), F32),
                        pltpu.VMEM((N_EXPERTS, 1), F32)],
        compiler_params=_params(2),
        name="mixer",
    )(x, ua, ua, ua, sga, sgb, yattn, mod, poolw, pscale, wa, wb, wout, g, b_, rwt, rb)


SEG_ALIGN = 8
SORTED_ROWS = TOP_K * MOE_TILE + SEG_ALIGN * N_EXPERTS
SEG_BITS = tuple(range(MOE_TILE.bit_length() - 1, SEG_ALIGN.bit_length() - 2, -1))


def _segment_copies(seg_ref, make_copy, action):
    def body(e, c):
        cnt = seg_ref[0, e, 0]
        src = seg_ref[0, e, 1]
        dst = seg_ref[0, e, 2]
        @pl.when(cnt > 0)
        def _():
            action(make_copy(pl.multiple_of(src, SEG_ALIGN), pl.multiple_of(dst, SEG_ALIGN),
                             pl.multiple_of(cnt, SEG_ALIGN)))
        return c

    lax.fori_loop(0, N_EXPERTS, body, 0)


def _sort_matrix(pos, n_sorted, transposed):
    if transposed:
        grid = lax.broadcasted_iota(jnp.int32, (pos.shape[0], n_sorted), 1)
        hit = grid == pos[:, 0:1]
        for k in range(1, TOP_K):
            hit = hit | (grid == pos[:, k:k + 1])
    else:
        grid = lax.broadcasted_iota(jnp.int32, (n_sorted, pos.shape[1]), 0)
        hit = grid == pos[0:1, :]
        for k in range(1, TOP_K):
            hit = hit | (grid == pos[k:k + 1, :])
    return jnp.where(hit, 1.0, 0.0).astype(BF16)


def _dispatch_kernel(pends_ref, padded_ref, n_used_ref,
                     seg_ref, seg_prev_ref, x1_ref, pos_ref, mod_ref, xb_ref,
                     sorted_ref, zero_ref, sem, seg_sems):
    td, d = x1_ref.shape
    n_blocks = xb_ref.shape[0] // EXPERT_ROWS

    @pl.when(pl.program_id(0) == 0)
    def _():
        zero_ref[...] = jnp.zeros_like(zero_ref)

        def zero_copy(row0):
            return pltpu.make_async_copy(
                zero_ref, xb_ref.at[pl.ds(pl.multiple_of(row0, EXPERT_ROWS), EXPERT_ROWS)], sem)

        def zero_blocks(action):
            def per_expert(e, c):
                @pl.when(padded_ref[e] > 0)
                def _():
                    action(zero_copy(pends_ref[e] - EXPERT_ROWS))
                return c

            def per_unused(r, c):
                action(zero_copy(r * EXPERT_ROWS))
                return c

            lax.fori_loop(0, N_EXPERTS, per_expert, 0)
            lax.fori_loop(n_used_ref[0], n_blocks, per_unused, 0)

        zero_blocks(lambda cp: cp.start())
        zero_blocks(lambda cp: cp.wait())

    sh2 = mod_ref[0, 3:4, :]
    sc2 = mod_ref[0, 4:5, :]
    u = (_ln_plain(x1_ref[...]) * (1.0 + sc2) + sh2).astype(BF16)
    rows = _dot(_sort_matrix(pos_ref[...], SORTED_ROWS, transposed=False), u)
    lo = lax.bitcast_convert_type(rows[:, :d // 2], jnp.uint32)
    hi = lax.bitcast_convert_type(rows[:, d // 2:], jnp.uint32)
    step = pl.program_id(0)
    slot = step % 2
    sorted_ref[slot] = hi | (lo >> 16)

    def copies_from(s):
        def make_copy(src, dst, size):
            return pltpu.make_async_copy(sorted_ref.at[s, pl.ds(src, size)],
                                         xb_ref.at[pl.ds(dst, size)], seg_sems.at[s])
        return make_copy

    _segment_copies(seg_ref, copies_from(slot), lambda cp: cp.start())

    @pl.when(step > 0)
    def _():
        _segment_copies(seg_prev_ref, copies_from(1 - slot), lambda cp: cp.wait())

    @pl.when(step == pl.num_programs(0) - 1)
    def _():
        _segment_copies(seg_ref, copies_from(slot), lambda cp: cp.wait())


def _dispatch_call(pends, padded, n_used, seg, x1_flat, pos, mod, n_rows, seq_len):
    n_tok, d = x1_flat.shape
    td = MOE_TILE
    return pl.pallas_call(
        _dispatch_kernel,
        grid_spec=pltpu.PrefetchScalarGridSpec(
            num_scalar_prefetch=3,
            grid=(n_tok // td,),
            in_specs=[pl.BlockSpec((1, N_EXPERTS, 3), lambda i, *_: (i, 0, 0), memory_space=pltpu.SMEM),
                      pl.BlockSpec((1, N_EXPERTS, 3), lambda i, *_: (jnp.maximum(i - 1, 0), 0, 0),
                                   memory_space=pltpu.SMEM),
                      pl.BlockSpec((td, d), lambda i, *_: (i, 0)),
                      pl.BlockSpec((TOP_K, td), lambda i, *_: (0, i)),
                      pl.BlockSpec((1, 6, d), lambda i, *_: ((i * td) // seq_len, 0, 0))],
            out_specs=pl.BlockSpec(memory_space=pl.ANY),
            scratch_shapes=[pltpu.VMEM((2, SORTED_ROWS, d // 2), jnp.uint32),
                            pltpu.VMEM((EXPERT_ROWS, d // 2), jnp.uint32),
                            pltpu.SemaphoreType.DMA,
                            pltpu.SemaphoreType.DMA((2,))]),
        out_shape=jax.ShapeDtypeStruct((n_rows, d // 2), jnp.uint32),
        compiler_params=_params(1, has_side_effects=True),
        name="dispatch",
    )(pends, padded, n_used, seg, seg, x1_flat, pos, mod)


CAST_ROWS = 128


def _expert_kernel(blk_e_ref, n_used_ref, first_ref, slot_ref, next_ref,
                   xb_ref, bup_ref, bdn_ref, wup_hbm, wdn_hbm, yb_ref,
                   wup32_ref, wdn32_ref, wup_ref, wdn_ref, sems):
    r = pl.program_id(0)
    used = r < n_used_ref[0]

    def weight_copies(e, slot):
        return (pltpu.make_async_copy(wup_hbm.at[e], wup32_ref.at[slot], sems.at[slot, 0]),
                pltpu.make_async_copy(wdn_hbm.at[e], wdn32_ref.at[slot], sems.at[slot, 1]))

    @pl.when(jnp.logical_not(used))
    def _():
        yb_ref[...] = jnp.zeros_like(yb_ref)

    @pl.when(used & (first_ref[r] == 1))
    def _():
        slot = slot_ref[r]

        @pl.when(r == 0)
        def _():
            for cp in weight_copies(blk_e_ref[r], slot):
                cp.start()

        for cp in weight_copies(blk_e_ref[r], slot):
            cp.wait()
        nxt = next_ref[r]

        @pl.when(nxt >= 0)
        def _():
            for cp in weight_copies(nxt, 1 - slot):
                cp.start()

        def cast(i, c):
            rows_ = pl.ds(pl.multiple_of(i * CAST_ROWS, CAST_ROWS), CAST_ROWS)
            wup_ref[rows_, :] = wup32_ref[slot, rows_, :].astype(BF16)
            wdn_ref[rows_, :] = wdn32_ref[slot, rows_, :].astype(BF16)
            return c

        lax.fori_loop(0, D_MODEL // CAST_ROWS, cast, 0)

    @pl.when(used)
    def _():
        words = xb_ref[...]
        lo = lax.bitcast_convert_type(words << 16, F32).astype(BF16)
        hi = lax.bitcast_convert_type(words & jnp.uint32(0xFFFF0000), F32).astype(BF16)
        half = D_MODEL // 2
        h = _dot(lo, wup_ref[:half, :]) + _dot(hi, wup_ref[half:, :]) + bup_ref[0]
        gate = jnp.minimum(h[:, :D_FF], SWIGLU_LIMIT)
        up = jnp.clip(h[:, D_FF:], -SWIGLU_LIMIT, SWIGLU_LIMIT)
        act = (up + 1.0) * (gate * jax.nn.sigmoid(SWIGLU_ALPHA * gate))
        yb_ref[...] = _dot(act.astype(BF16), wdn_ref[...]) + bdn_ref[0]


def _expert_call(blk_e, n_used, blk_first, blk_slot, blk_next, xb, wup, bup, wdn, bdn):
    n_rows = xb.shape[0]
    d = D_MODEL
    assert D_FF == d
    rows = EXPERT_ROWS
    last = lambda r, be, nu, *_: jnp.maximum(jnp.minimum(r, nu[0] - 1), 0)
    exp3 = lambda r, be, nu, *_: (be[last(r, be, nu)], 0, 0)
    return pl.pallas_call(
        _expert_kernel,
        grid_spec=pltpu.PrefetchScalarGridSpec(
            num_scalar_prefetch=5,
            grid=(n_rows // rows,),
            in_specs=[pl.BlockSpec((rows, d // 2), lambda r, be, nu, *_: (last(r, be, nu), 0)),
                      pl.BlockSpec((1, 1, 2 * D_FF), exp3),
                      pl.BlockSpec((1, 1, d), exp3),
                      pl.BlockSpec(memory_space=pl.ANY),
                      pl.BlockSpec(memory_space=pl.ANY)],
            out_specs=pl.BlockSpec((rows, d), lambda r, *_: (r, 0)),
            scratch_shapes=[pltpu.VMEM((2, d, 2 * D_FF), F32),
                            pltpu.VMEM((2, D_FF, d), F32),
                            pltpu.VMEM((d, 2 * D_FF), BF16),
                            pltpu.VMEM((D_FF, d), BF16),
                            pltpu.SemaphoreType.DMA((2, 2))]),
        out_shape=jax.ShapeDtypeStruct((n_rows, d), F32),
        compiler_params=_params(1),
        name="experts",
    )(blk_e, n_used, blk_first, blk_slot, blk_next, xb, bup, bdn, wup, wdn)


def _combine_kernel(seg_ref, seg_next_ref, x1_ref, pos_ref, post_ref, gates_ref, mod_ref, g_ref, b_ref,
                    yb_ref, o_ref, ybuf_ref, sems):
    tc = x1_ref.shape[0]
    n_sorted = SORTED_ROWS
    step = pl.program_id(0)
    slot = step % 2

    def copies_into(s):
        def make_copy(src, dst, size):
            return pltpu.make_async_copy(yb_ref.at[pl.ds(dst, size)],
                                         ybuf_ref.at[s, pl.ds(src, size)], sems.at[s])
        return make_copy

    def fetch(tile_seg_ref, s):
        ybuf_ref[s, TOP_K * tc:, :] = jnp.zeros((n_sorted - TOP_K * tc, ybuf_ref.shape[2]), F32)
        _segment_copies(tile_seg_ref, copies_into(s), lambda cp: cp.start())

    @pl.when(step == 0)
    def _():
        fetch(seg_ref, slot)

    @pl.when(step + 1 < pl.num_programs(0))
    def _():
        fetch(seg_next_ref, 1 - slot)

    pos = pos_ref[...]
    gates = gates_ref[...]
    grid = lax.broadcasted_iota(jnp.int32, (n_sorted, tc), 0)
    gsel = jnp.where(grid == pos[0:1, :], gates[0:1, :], 0.0)
    for k in range(1, TOP_K):
        gsel = gsel + jnp.where(grid == pos[k:k + 1, :], gates[k:k + 1, :], 0.0)
    gate_col = jnp.sum(gsel, axis=1, keepdims=True)
    unsort = _sort_matrix(post_ref[...], n_sorted, transposed=True)

    _segment_copies(seg_ref, copies_into(slot), lambda cp: cp.wait())

    yg = ybuf_ref[slot] * gate_col
    y_hi = yg.astype(BF16)
    y_lo = (yg - y_hi.astype(F32)).astype(BF16)
    h2 = _dot(unsort, y_hi) + _dot(unsort, y_lo)
    g2 = mod_ref[0, 5:6, :]
    o_ref[...] = _ln_plain(DEEPNORM_ALPHA * x1_ref[...] + g2 * h2) * g_ref[...] + b_ref[...]


def _combine_call(seg, x1_flat, pos, pos_t, gates, mod, g, b_, yb, seq_len):
    n_tok, d = x1_flat.shape
    tc = MOE_TILE
    c2 = lambda i: (0, 0)
    n_tiles = n_tok // tc
    return pl.pallas_call(
        _combine_kernel,
        grid=(n_tiles,),
        in_specs=[pl.BlockSpec((1, N_EXPERTS, 3), lambda i: (i, 0, 0), memory_space=pltpu.SMEM),
                  pl.BlockSpec((1, N_EXPERTS, 3), lambda i: (jnp.minimum(i + 1, n_tiles - 1), 0, 0),
                               memory_space=pltpu.SMEM),
                  pl.BlockSpec((tc, d), lambda i: (i, 0)),
                  pl.BlockSpec((TOP_K, tc), lambda i: (0, i)),
                  pl.BlockSpec((tc, TOP_K), lambda i: (i, 0)),
                  pl.BlockSpec((TOP_K, tc), lambda i: (0, i)),
                  pl.BlockSpec((1, 6, d), lambda i: ((i * tc) // seq_len, 0, 0)),
                  pl.BlockSpec((1, d), c2),
                  pl.BlockSpec((1, d), c2),
                  pl.BlockSpec(memory_space=pl.ANY)],
        out_specs=pl.BlockSpec((tc, d), lambda i: (i, 0)),
        out_shape=jax.ShapeDtypeStruct((n_tok, d), F32),
        scratch_shapes=[pltpu.VMEM((2, SORTED_ROWS, d), F32), pltpu.SemaphoreType.DMA((2,))],
        compiler_params=_params(1),
        name="combine",
    )(seg, seg, x1_flat, pos, pos_t, gates, mod, g, b_, yb)


def _rope_tables(seq_len):
    inv_freq = ROPE_THETA ** (-jnp.arange(0, HEAD_DIM, 2, dtype=F32) / HEAD_DIM)
    ang = jnp.arange(seq_len, dtype=F32)[:, None] * inv_freq[None, :]
    cos, sin = lax.optimization_barrier((jnp.cos(ang), jnp.sin(ang)))
    zero = jnp.zeros_like(sin)
    reps = HEAD_W // HEAD_DIM
    cos_t = jnp.tile(jnp.concatenate([cos, cos], axis=1), (1, reps))
    sina_t = jnp.tile(jnp.concatenate([-sin, zero], axis=1), (1, reps))
    sinb_t = jnp.tile(jnp.concatenate([zero, sin], axis=1), (1, reps))
    return cos_t, sina_t, sinb_t


def _encode(x, mod, w):
    b, s, d = x.shape
    n_tok = b * s
    cos, sina, sinb = w["rope"]
    ua, q, k, vt, sga, sgb = _inproj_call(x, mod, cos, sina, sinb, w["wcat"], w["wvt"])
    yattn = _attn_call(q, k, vt, w["lq1"], w["lk1"], w["lq2"], w["lk2"], w["subw"])
    x1, pos, gates, seg, cnt = _mixer_call(
        x, ua, sga, sgb, yattn, mod, w["poolw"], w["pscale"], w["wa"], w["wb"], w["wout"],
        w["ln1_g"], w["ln1_b"], w["rwt"], w["rb"])

    rows = EXPERT_ROWS
    counts = cnt[:, 0].astype(jnp.int32)
    padded = (counts + rows - 1) // rows * rows
    pends = jnp.cumsum(padded)
    pstarts = pends - padded
    seg = seg.at[:, :, 2].add(pstarts[None, :])
    n_tiles = n_tok // MOE_TILE
    n_blocks = (n_tiles * SORTED_ROWS) // rows + N_EXPERTS
    blk_row0 = jnp.arange(n_blocks, dtype=jnp.int32) * rows
    blk_e = jnp.minimum(jnp.sum((pends[None, :] <= blk_row0[:, None]).astype(jnp.int32), axis=1),
                        N_EXPERTS - 1)
    n_used = (pends[-1:] // rows).astype(jnp.int32)
    blk_ids = jnp.arange(n_blocks, dtype=jnp.int32)
    blk_first = ((blk_ids == 0) | (blk_e != jnp.roll(blk_e, 1))).astype(jnp.int32)
    blk_slot = (jnp.cumsum(blk_first) - 1) % 2
    e_ids = jnp.arange(N_EXPERTS, dtype=jnp.int32)
    later = (e_ids[None, :] > e_ids[:, None]) & (padded[None, :] > 0)
    next_present = jnp.min(jnp.where(later, e_ids[None, :], N_EXPERTS), axis=1)
    next_present = jnp.where(next_present == N_EXPERTS, -1, next_present)
    blk_next = jnp.sum(jnp.where(blk_e[:, None] == e_ids[None, :], next_present[None, :], 0), axis=1)

    x1_flat = x1.reshape(n_tok, d)
    xb = _dispatch_call(pends.astype(jnp.int32), padded, n_used, seg, x1_flat, pos, mod, n_blocks * rows, s)
    yb = _expert_call(blk_e, n_used, blk_first, blk_slot.astype(jnp.int32), blk_next.astype(jnp.int32),
                      xb, w["wup"], w["bup"], w["wdn"], w["bdn"])
    out = _combine_call(seg, x1_flat, pos, pos.T, gates, mod, w["ln2_g"], w["ln2_b"], yb, s)
    return out.reshape(b, s, d)


def kernel(x_prompt, x_sample, c_prompt, c_sample, w_ada, b_ada, w_in, pool_w, pool_scale, w_branch_a, w_branch_b, lambda_q1, lambda_k1, lambda_q2, lambda_k2, subln_w, w_out, ln1_g, ln1_b, router_w, router_b, exp_w_up, exp_b_up, exp_w_down, exp_b_down, ln2_g, ln2_b):
    d = D_MODEL
    w_in0 = w_in[0]
    seg = lambda j: w_in0[:, j * d:(j + 1) * d]
    w = {
        "wcat": jnp.concatenate([seg(0), seg(1), seg(2), seg(4), seg(5)], axis=1).astype(BF16),
        "wvt": seg(3).T.astype(BF16),
        "lq1": lambda_q1, "lk1": lambda_k1, "lq2": lambda_q2, "lk2": lambda_k2,
        "subw": subln_w[0].reshape(HEAD_W, 1),
        "poolw": pool_w[0].astype(BF16),
        "pscale": pool_scale,
        "wa": w_branch_a[0].astype(BF16),
        "wb": w_branch_b[0].astype(BF16),
        "wout": w_out[0].astype(BF16),
        "ln1_g": ln1_g, "ln1_b": ln1_b,
        "rwt": router_w[0].T.astype(BF16),
        "rb": router_b[0].reshape(N_EXPERTS, 1),
        "wup": exp_w_up[0],
        "bup": exp_b_up[0].reshape(N_EXPERTS, 1, 2 * D_FF),
        "wdn": exp_w_down[0],
        "bdn": exp_b_down[0].reshape(N_EXPERTS, 1, d),
        "ln2_g": ln2_g, "ln2_b": ln2_b,
        "rope": _rope_tables(max(x_prompt.shape[1], x_sample.shape[1])),
    }
    nb_p, nb_s = c_prompt.shape[0], c_sample.shape[0]
    pad = (-(nb_p + nb_s)) % 8
    c_all = jnp.concatenate([c_prompt, c_sample, jnp.zeros((pad, d), F32)], axis=0)
    mod = _mod_call(c_all, w_ada[0], b_ada[0]).reshape(-1, 6, d)
    y_prompt = _encode(x_prompt, mod[:nb_p], w)
    y_sample = _encode(x_sample, mod[nb_p:nb_p + nb_s], w)
    return (y_prompt, y_sample)
```

```python
import functools

import jax
import jax.numpy as jnp
from jax import lax
from jax.experimental import pallas as pl
from jax.experimental.pallas import tpu as pltpu

F32 = jnp.float32
BF16 = jnp.bfloat16

D_MODEL = 1024
N_HEADS = 8
HEAD_DIM = 64
HEAD_W = 2 * HEAD_DIM
ROPE_THETA = 10000.0
POOL_WINDOWS = (2, 4, 8, 16)
POOL_GC = 256
POOL_HALO = 8
N_EXPERTS = 32
TOP_K = 4
D_FF = 1024
SWIGLU_LIMIT = 7.0
SWIGLU_ALPHA = 1.702
LN_EPS = 1e-5
DEPTH = 1
DEEPNORM_ALPHA = (2.0 * DEPTH) ** 0.25
LAMBDA_INIT = 0.8 - 0.6 * 1.0

VMEM_LIMIT_V7X = 56 * 1024 * 1024

TOKEN_TILE = 256
MOE_TILE = TOKEN_TILE
ATTN_Q_TILE = 1024
ATTN_K_CHUNK = 1024
LOG2_E = 1.4426950408889634
EXPERT_ROWS = 256


def _params(n_axes, **kw):
    return pltpu.CompilerParams(dimension_semantics=("arbitrary",) * n_axes,
                                vmem_limit_bytes=VMEM_LIMIT_V7X, **kw)


def _ln_plain(x):
    mu = jnp.mean(x, axis=-1, keepdims=True)
    xc = x - mu
    var = jnp.mean(xc * xc, axis=-1, keepdims=True)
    return xc * lax.rsqrt(var + LN_EPS)


def _dot(a, b):
    return jnp.dot(a, b, preferred_element_type=F32)


def _dot_nt(a, b, **kw):
    return lax.dot_general(a, b, (((1,), (1,)), ((), ())), preferred_element_type=F32, **kw)


def _mod_kernel(c_ref, w_ref, b_ref, o_ref):
    c = c_ref[...]
    s = c * jax.nn.sigmoid(c)
    o_ref[...] = jnp.dot(s, w_ref[...], preferred_element_type=F32,
                         precision=lax.Precision.HIGHEST) + b_ref[...]


def _mod_call(c, w_ada, b_ada):
    rows, d = c.shape
    n_out = w_ada.shape[1]
    return pl.pallas_call(
        _mod_kernel,
        grid=(n_out // d,),
        in_specs=[pl.BlockSpec((rows, d), lambda j: (0, 0)),
                  pl.BlockSpec((d, d), lambda j: (0, j)),
                  pl.BlockSpec((1, d), lambda j: (0, j))],
        out_specs=pl.BlockSpec((rows, d), lambda j: (0, j)),
        out_shape=jax.ShapeDtypeStruct((rows, n_out), F32),
        compiler_params=_params(1),
        name="mod",
    )(c, w_ada, b_ada.reshape(1, n_out))


def _inproj_kernel(x_ref, mod_ref, cos_ref, sina_ref, sinb_ref, wcat_ref, wvt_ref,
                   ua_ref, q_ref, k_ref, vt_ref, sga_ref, sgb_ref):
    d = D_MODEL
    x = x_ref[0]
    sh1 = mod_ref[0, 0:1, :]
    sc1 = mod_ref[0, 1:2, :]
    u = (_ln_plain(x) * (1.0 + sc1) + sh1).astype(BF16)

    def proj(j):
        return _dot(u, wcat_ref[:, j * d:(j + 1) * d])

    reps = d // HEAD_W
    cos = jnp.concatenate([cos_ref[...]] * reps, axis=1)
    sina = jnp.concatenate([sina_ref[...]] * reps, axis=1)
    sinb = jnp.concatenate([sinb_ref[...]] * reps, axis=1)

    def rope(t):
        half = HEAD_DIM // 2
        return t * cos + pltpu.roll(t, d - half, 1) * sina + pltpu.roll(t, half, 1) * sinb

    ua_ref[0] = proj(0)
    q_ref[0] = (rope(proj(1)) * (HEAD_DIM ** -0.5 * LOG2_E)).astype(BF16)
    k_ref[0] = rope(proj(2)).astype(BF16)
    vt_ref[0] = _dot_nt(wvt_ref[...], u).astype(BF16)
    sga_ref[0] = jax.nn.sigmoid(proj(3))
    sgb_ref[0] = jax.nn.sigmoid(proj(4))


def _inproj_call(x, mod, cos, sina, sinb, wcat, wvt):
    b, s, d = x.shape
    tm = min(TOKEN_TILE, s)
    row = lambda bi, i: (bi, i, 0)
    const2 = lambda bi, i: (0, 0)
    tab = pl.BlockSpec((tm, HEAD_W), lambda bi, i: (i, 0))
    return pl.pallas_call(
        _inproj_kernel,
        grid=(b, s // tm),
        in_specs=[pl.BlockSpec((1, tm, d), row),
                  pl.BlockSpec((1, 6, d), lambda bi, i: (bi, 0, 0)),
                  tab, tab, tab,
                  pl.BlockSpec(wcat.shape, const2),
                  pl.BlockSpec(wvt.shape, const2)],
        out_specs=[pl.BlockSpec((1, tm, d), row),
                   pl.BlockSpec((1, tm, d), row),
                   pl.BlockSpec((1, tm, d), row),
                   pl.BlockSpec((1, d, tm), lambda bi, i: (bi, 0, i)),
                   pl.BlockSpec((1, tm, d), row),
                   pl.BlockSpec((1, tm, d), row)],
        out_shape=[jax.ShapeDtypeStruct((b, s, d), F32),
                   jax.ShapeDtypeStruct((b, s, d), BF16),
                   jax.ShapeDtypeStruct((b, s, d), BF16),
                   jax.ShapeDtypeStruct((b, d, s), BF16),
                   jax.ShapeDtypeStruct((b, s, d), F32),
                   jax.ShapeDtypeStruct((b, s, d), F32)],
        compiler_params=_params(2),
        name="inproj",
    )(x, mod, cos, sina, sinb, wcat, wvt)


def _attn_kernel(q_ref, k_ref, vt_ref, lq1_ref, lk1_ref, lq2_ref, lk2_ref, subw_ref,
                 o_ref, qt_ref, sa_ref, sb_ref, o1_ref, o2_ref, *, k_chunk):
    tq = q_ref.shape[1]
    s_len = k_ref.shape[1]
    q = q_ref[0]
    lane = lax.broadcasted_iota(jnp.int32, q.shape, 1)
    zero = jnp.zeros_like(q)
    qt_ref[0] = jnp.where(lane < HEAD_DIM, q, zero).T
    qt_ref[1] = jnp.where(lane >= HEAD_DIM, q, zero).T
    o1_ref[...] = jnp.zeros_like(o1_ref)
    o2_ref[...] = jnp.zeros_like(o2_ref)
    n_chunks = s_len // k_chunk

    def scores(c, s_ref):
        start = pl.multiple_of(c * k_chunk, k_chunk)
        kc = k_ref[0, pl.ds(start, k_chunk), :]
        s1 = _dot(kc, qt_ref[0])
        s2 = _dot(kc, qt_ref[1])
        s_ref[0] = s1
        s_ref[1] = s2
        return jnp.max(s1, axis=0, keepdims=True), jnp.max(s2, axis=0, keepdims=True)

    def accumulate(c, s_ref, cmax, state):
        start = pl.multiple_of(c * k_chunk, k_chunk)
        vtc = vt_ref[0, :, pl.ds(start, k_chunk)]
        out = []
        for j, acc_ref in enumerate((o1_ref, o2_ref)):
            m, l = state[j]
            m_new = jnp.maximum(m, cmax[j])
            alpha = jnp.exp2(m - m_new)
            p = jnp.exp2(s_ref[j] - m_new)
            l_new = alpha * l + jnp.sum(p, axis=0, keepdims=True)
            acc_ref[...] = alpha * acc_ref[...] + _dot(vtc, p.astype(BF16))
            out.append((m_new, l_new))
        return tuple(out)

    def body(i, carry):
        cmax_a, m = carry
        cmax_b = scores(2 * i + 1, sb_ref)
        m = accumulate(2 * i, sa_ref, cmax_a, m)
        cmax_a = scores(2 * i + 2, sa_ref)
        m = accumulate(2 * i + 1, sb_ref, cmax_b, m)
        return cmax_a, m

    init = (jnp.full((1, tq), -jnp.inf, F32), jnp.zeros((1, tq), F32))
    cmax_a, m = lax.fori_loop(0, n_chunks // 2 - 1, body, (scores(0, sa_ref), (init, init)))
    cmax_b = scores(n_chunks - 1, sb_ref)
    m = accumulate(n_chunks - 2, sa_ref, cmax_a, m)
    (_, l1), (_, l2) = accumulate(n_chunks - 1, sb_ref, cmax_b, m)

    lam = (jnp.exp(jnp.sum(lq1_ref[...] * lk1_ref[...], axis=1, keepdims=True))
           - jnp.exp(jnp.sum(lq2_ref[...] * lk2_ref[...], axis=1, keepdims=True))
           + LAMBDA_INIT)
    o = o1_ref[...] / l1 - lam * (o2_ref[...] / l2)
    o = o * lax.rsqrt(jnp.mean(o * o, axis=0, keepdims=True) + LN_EPS)
    o = o * subw_ref[...] * (1.0 - LAMBDA_INIT)
    o_ref[0] = o.T.astype(o_ref.dtype)


def _attn_call(q, k, vt, lq1, lk1, lq2, lk2, subw):
    b, s, d = q.shape
    tq = min(ATTN_Q_TILE, s)
    kc = min(ATTN_K_CHUNK, s // 4)
    assert s % (2 * kc) == 0 and s % tq == 0
    lam_spec = pl.BlockSpec((1, HEAD_DIM), lambda bi, h, i: (0, 0))
    return pl.pallas_call(
        functools.partial(_attn_kernel, k_chunk=kc),
        grid=(b, N_HEADS, s // tq),
        in_specs=[pl.BlockSpec((1, tq, HEAD_W), lambda bi, h, i: (bi, i, h)),
                  pl.BlockSpec((1, s, HEAD_W), lambda bi, h, i: (bi, 0, h)),
                  pl.BlockSpec((1, HEAD_W, s), lambda bi, h, i: (bi, h, 0)),
                  lam_spec, lam_spec, lam_spec, lam_spec,
                  pl.BlockSpec((HEAD_W, 1), lambda bi, h, i: (0, 0))],
        out_specs=pl.BlockSpec((1, tq, HEAD_W), lambda bi, h, i: (bi, i, h)),
        out_shape=jax.ShapeDtypeStruct((b, s, d), BF16),
        scratch_shapes=[pltpu.VMEM((2, HEAD_W, tq), BF16),
                        pltpu.VMEM((2, kc, tq), F32), pltpu.VMEM((2, kc, tq), F32),
                        pltpu.VMEM((HEAD_W, tq), F32),
                        pltpu.VMEM((HEAD_W, tq), F32)],
        compiler_params=_params(3),
        name="attn",
    )(q, k, vt, lq1, lk1, lq2, lk2, subw)


def _mixer_kernel(x_ref, ua_ref, uprev_ref, unext_ref, sga_ref, sgb_ref, yattn_ref, mod_ref,
                  poolw_ref, pscale_ref, wa_ref, wb_ref, wout_ref, g_ref, b_ref, rwt_ref, rb_ref,
                  x1_ref, pos_ref, gate_ref, seg_ref, cnt_ref,
                  ext_ref, part_ref, run_ref, *, seq_len):
    tm = x_ref.shape[1]
    halo = POOL_HALO
    bi = pl.program_id(0)
    i = pl.program_id(1)
    n_i = pl.num_programs(1)

    @pl.when((bi == 0) & (i == 0))
    def _():
        run_ref[...] = jnp.zeros_like(run_ref)

    ua = ua_ref[0]
    ext_ref[0:halo, :] = jnp.where(i > 0, uprev_ref[0], 0.0)
    ext_ref[halo:halo + tm, :] = ua
    ext_ref[halo + tm:, :] = jnp.where(i < n_i - 1, unext_ref[0], 0.0)
    pos = i * tm + lax.broadcasted_iota(jnp.int32, (tm, 1), 0)
    mixed = []
    for g, w in enumerate(POOL_WINDOWS):
        cols = slice(g * POOL_GC, (g + 1) * POOL_GC)
        src, src_cols, span, rows = ext_ref, cols, 1, tm + 2 * halo
        level = 0
        while 2 * span < w:
            rows -= span
            part_ref[level, 0:rows, :] = src[0:rows, src_cols] + src[span:span + rows, src_cols]
            src, src_cols, span = part_ref.at[level], slice(None), 2 * span
            level += 1
        first = halo - w // 2
        win = src[first:first + tm, src_cols] + src[first + span:first + span + tm, src_cols]
        lo = jnp.maximum(pos - w // 2, 0)
        hi = jnp.minimum(pos + w // 2 - 1, seq_len - 1)
        cnt = (hi - lo + 1).astype(F32)
        pooled = win / cnt - ua[:, cols]
        mixed.append(_dot(pooled.astype(BF16), poolw_ref[g]))
    mixed = jnp.concatenate(mixed, axis=1) * pscale_ref[...]
    ya = _dot(mixed.astype(BF16), wa_ref[...])

    yb = _dot(yattn_ref[0], wb_ref[...])
    merged = sga_ref[0] * ya + sgb_ref[0] * yb
    h = _dot(merged.astype(BF16), wout_ref[...])
    g1 = mod_ref[0, 2:3, :]
    x1 = _ln_plain(DEEPNORM_ALPHA * x_ref[0] + g1 * h) * g_ref[...] + b_ref[...]
    x1_ref[0] = x1

    sh2 = mod_ref[0, 3:4, :]
    sc2 = mod_ref[0, 4:5, :]
    u2 = _ln_plain(x1) * (1.0 + sc2) + sh2
    logits = _dot_nt(rwt_ref[...], u2.astype(BF16)) + rb_ref[...]
    eio = lax.broadcasted_iota(jnp.int32, logits.shape, 0)
    vals, sels = [], []
    l = logits
    for _ in range(TOP_K):
        mv = jnp.max(l, axis=0, keepdims=True)
        idx = jnp.min(jnp.where(l == mv, eio, N_EXPERTS), axis=0, keepdims=True)
        sel = eio == idx
        vals.append(mv)
        sels.append(sel)
        l = jnp.where(sel, -jnp.inf, l)
    ex = [jnp.exp(v - vals[0]) for v in vals]
    den = ex[0] + ex[1] + ex[2] + ex[3]
    gate_ref[...] = jnp.concatenate([e / den for e in ex], axis=0)

    member = jnp.zeros(logits.shape, F32)
    for sel in sels:
        member = member + sel.astype(F32)
    member_b = member.astype(BF16)
    tok_before = (lax.broadcasted_iota(jnp.int32, (tm, tm), 0)
                  < lax.broadcasted_iota(jnp.int32, (tm, tm), 1)).astype(BF16)
    exp_before = (lax.broadcasted_iota(jnp.int32, (N_EXPERTS, N_EXPERTS), 1)
                  < lax.broadcasted_iota(jnp.int32, (N_EXPERTS, N_EXPERTS), 0)).astype(BF16)
    rank_in_tile = _dot(member_b, tok_before)
    seg_cnt = jnp.sum(member, axis=1, keepdims=True)
    seg_rows = jnp.ceil(seg_cnt * (1.0 / SEG_ALIGN)) * SEG_ALIGN
    seg_rows_b = jnp.broadcast_to(seg_rows, (N_EXPERTS, 128)).astype(BF16)
    seg_off = _dot(exp_before, seg_rows_b)[:, 0:1]
    base = seg_off + rank_in_tile
    pos = [jnp.sum(jnp.where(sel, base, 0.0), axis=0, keepdims=True) for sel in sels]
    pos_ref[...] = jnp.concatenate(pos, axis=0).astype(jnp.int32)
    run = run_ref[...]
    seg_ref[0] = jnp.concatenate([seg_rows, seg_off, run], axis=1).astype(jnp.int32)
    run_ref[...] = run + seg_rows
    cnt_ref[...] = jnp.broadcast_to(run + seg_rows, cnt_ref.shape)


def _mixer_call(x, ua, sga, sgb, yattn, mod, poolw, pscale, wa, wb, wout, g, b_, rwt, rb):
    b, s, d = x.shape
    tm = min(TOKEN_TILE, s)
    n_i = s // tm
    hb = tm // POOL_HALO
    n_hb = s // POOL_HALO
    row = lambda bi, i: (bi, i, 0)
    c2 = lambda bi, i: (0, 0)
    c3 = lambda bi, i: (0, 0, 0)
    tokcol = lambda bi, i: (0, bi * n_i + i)
    n_tok = b * s
    return pl.pallas_call(
        functools.partial(_mixer_kernel, seq_len=s),
        grid=(b, n_i),
        in_specs=[pl.BlockSpec((1, tm, d), row),
                  pl.BlockSpec((1, tm, d), row),
                  pl.BlockSpec((1, POOL_HALO, d), lambda bi, i: (bi, jnp.maximum(i * hb - 1, 0), 0)),
                  pl.BlockSpec((1, POOL_HALO, d), lambda bi, i: (bi, jnp.minimum((i + 1) * hb, n_hb - 1), 0)),
                  pl.BlockSpec((1, tm, d), row),
                  pl.BlockSpec((1, tm, d), row),
                  pl.BlockSpec((1, tm, d), row),
                  pl.BlockSpec((1, 6, d), lambda bi, i: (bi, 0, 0)),
                  pl.BlockSpec(poolw.shape, c3),
                  pl.BlockSpec((1, d), c2),
                  pl.BlockSpec((d, d), c2),
                  pl.BlockSpec((d, d), c2),
                  pl.BlockSpec((d, d), c2),
                  pl.BlockSpec((1, d), c2),
                  pl.BlockSpec((1, d), c2),
                  pl.BlockSpec((N_EXPERTS, d), c2),
                  pl.BlockSpec((N_EXPERTS, 1), c2)],
        out_specs=[pl.BlockSpec((1, tm, d), row),
                   pl.BlockSpec((TOP_K, tm), tokcol),
                   pl.BlockSpec((TOP_K, tm), tokcol),
                   pl.BlockSpec((1, N_EXPERTS, 3), lambda bi, i: (bi * n_i + i, 0, 0)),
                   pl.BlockSpec((N_EXPERTS, 128), c2)],
        out_shape=[jax.ShapeDtypeStruct((b, s, d), F32),
                   jax.ShapeDtypeStruct((TOP_K, n_tok), jnp.int32),
                   jax.ShapeDtypeStruct((TOP_K, n_tok), F32),
                   jax.ShapeDtypeStruct((b * n_i, N_EXPERTS, 3), jnp.int32),
                   jax.ShapeDtypeStruct((N_EXPERTS, 128), F32)],
        scratch_shapes=[pltpu.VMEM((tm + 2 * POOL_HALO, d), F32),
                        pltpu.VMEM((max(POOL_WINDOWS).bit_length() - 2, tm + 2 * POOL_HALO, POOL_GC), F32),
                        pltpu.VMEM((N_EXPERTS, 1), F32)],
        compiler_params=_params(2),
        name="mixer",
    )(x, ua, ua, ua, sga, sgb, yattn, mod, poolw, pscale, wa, wb, wout, g, b_, rwt, rb)


SEG_ALIGN = 8
SORTED_ROWS = TOP_K * MOE_TILE + SEG_ALIGN * N_EXPERTS


def _segment_copies(seg_ref, make_copy, action):
    def body(e, c):
        cnt = seg_ref[0, e, 0]
        src = seg_ref[0, e, 1]
        dst = seg_ref[0, e, 2]
        @pl.when(cnt > 0)
        def _():
            action(make_copy(pl.multiple_of(src, SEG_ALIGN), pl.multiple_of(dst, SEG_ALIGN),
                             pl.multiple_of(cnt, SEG_ALIGN)))
        return c

    lax.fori_loop(0, N_EXPERTS, body, 0)


def _sort_matrix(pos, n_sorted, transposed):
    one_hot = 0.0
    if transposed:
        grid = lax.broadcasted_iota(jnp.int32, (pos.shape[0], n_sorted), 1)
        for k in range(TOP_K):
            one_hot = jnp.where(grid == pos[:, k:k + 1], 1.0, one_hot)
    else:
        grid = lax.broadcasted_iota(jnp.int32, (n_sorted, pos.shape[1]), 0)
        for k in range(TOP_K):
            one_hot = jnp.where(grid == pos[k:k + 1, :], 1.0, one_hot)
    return one_hot.astype(BF16)


def _dispatch_kernel(pends_ref, padded_ref, n_used_ref,
                     seg_ref, seg_prev_ref, x1_ref, pos_ref, mod_ref, xb_ref,
                     sorted_ref, zero_ref, sem, seg_sems):
    td, d = x1_ref.shape
    n_blocks = xb_ref.shape[0] // EXPERT_ROWS

    @pl.when(pl.program_id(0) == 0)
    def _():
        zero_ref[...] = jnp.zeros_like(zero_ref)

        def zero_copy(row0):
            return pltpu.make_async_copy(
                zero_ref, xb_ref.at[pl.ds(pl.multiple_of(row0, EXPERT_ROWS), EXPERT_ROWS)], sem)

        def zero_blocks(action):
            def per_expert(e, c):
                @pl.when(padded_ref[e] > 0)
                def _():
                    action(zero_copy(pends_ref[e] - EXPERT_ROWS))
                return c

            def per_unused(r, c):
                action(zero_copy(r * EXPERT_ROWS))
                return c

            lax.fori_loop(0, N_EXPERTS, per_expert, 0)
            lax.fori_loop(n_used_ref[0], n_blocks, per_unused, 0)

        zero_blocks(lambda cp: cp.start())
        zero_blocks(lambda cp: cp.wait())

    sh2 = mod_ref[0, 3:4, :]
    sc2 = mod_ref[0, 4:5, :]
    u = (_ln_plain(x1_ref[...]) * (1.0 + sc2) + sh2).astype(BF16)
    rows = _dot(_sort_matrix(pos_ref[...], SORTED_ROWS, transposed=False), u)
    lo = lax.bitcast_convert_type(rows[:, :d // 2], jnp.uint32)
    hi = lax.bitcast_convert_type(rows[:, d // 2:], jnp.uint32)
    step = pl.program_id(0)
    slot = step % 2
    sorted_ref[slot] = hi | (lo >> 16)

    def copies_from(s):
        def make_copy(src, dst, size):
            return pltpu.make_async_copy(sorted_ref.at[s, pl.ds(src, size)],
                                         xb_ref.at[pl.ds(dst, size)], seg_sems.at[s])
        return make_copy

    _segment_copies(seg_ref, copies_from(slot), lambda cp: cp.start())

    @pl.when(step > 0)
    def _():
        _segment_copies(seg_prev_ref, copies_from(1 - slot), lambda cp: cp.wait())

    @pl.when(step == pl.num_programs(0) - 1)
    def _():
        _segment_copies(seg_ref, copies_from(slot), lambda cp: cp.wait())


def _dispatch_call(pends, padded, n_used, seg, x1_flat, pos, mod, n_rows, seq_len):
    n_tok, d = x1_flat.shape
    td = MOE_TILE
    return pl.pallas_call(
        _dispatch_kernel,
        grid_spec=pltpu.PrefetchScalarGridSpec(
            num_scalar_prefetch=3,
            grid=(n_tok // td,),
            in_specs=[pl.BlockSpec((1, N_EXPERTS, 3), lambda i, *_: (i, 0, 0), memory_space=pltpu.SMEM),
                      pl.BlockSpec((1, N_EXPERTS, 3), lambda i, *_: (jnp.maximum(i - 1, 0), 0, 0),
                                   memory_space=pltpu.SMEM),
                      pl.BlockSpec((td, d), lambda i, *_: (i, 0)),
                      pl.BlockSpec((TOP_K, td), lambda i, *_: (0, i)),
                      pl.BlockSpec((1, 6, d), lambda i, *_: ((i * td) // seq_len, 0, 0))],
            out_specs=pl.BlockSpec(memory_space=pl.ANY),
            scratch_shapes=[pltpu.VMEM((2, SORTED_ROWS, d // 2), jnp.uint32),
                            pltpu.VMEM((EXPERT_ROWS, d // 2), jnp.uint32),
                            pltpu.SemaphoreType.DMA,
                            pltpu.SemaphoreType.DMA((2,))]),
        out_shape=jax.ShapeDtypeStruct((n_rows, d // 2), jnp.uint32),
        compiler_params=_params(1, has_side_effects=True),
        name="dispatch",
    )(pends, padded, n_used, seg, seg, x1_flat, pos, mod)


CAST_ROWS = 128


def _expert_kernel(blk_e_ref, n_used_ref, first_ref, slot_ref, next_ref,
                   xb_ref, bup_ref, bdn_ref, wup_hbm, wdn_hbm, yb_ref,
                   wup32_ref, wdn32_ref, wup_ref, wdn_ref, sems):
    r = pl.program_id(0)
    used = r < n_used_ref[0]

    def weight_copies(e, slot):
        return (pltpu.make_async_copy(wup_hbm.at[e], wup32_ref.at[slot], sems.at[slot, 0]),
                pltpu.make_async_copy(wdn_hbm.at[e], wdn32_ref.at[slot], sems.at[slot, 1]))

    @pl.when(jnp.logical_not(used))
    def _():
        yb_ref[...] = jnp.zeros_like(yb_ref)

    @pl.when(used & (first_ref[r] == 1))
    def _():
        slot = slot_ref[r]

        @pl.when(r == 0)
        def _():
            for cp in weight_copies(blk_e_ref[r], slot):
                cp.start()

        for cp in weight_copies(blk_e_ref[r], slot):
            cp.wait()
        nxt = next_ref[r]

        @pl.when(nxt >= 0)
        def _():
            for cp in weight_copies(nxt, 1 - slot):
                cp.start()

        def cast(i, c):
            rows_ = pl.ds(pl.multiple_of(i * CAST_ROWS, CAST_ROWS), CAST_ROWS)
            wup_ref[rows_, :] = wup32_ref[slot, rows_, :].astype(BF16)
            wdn_ref[rows_, :] = wdn32_ref[slot, rows_, :].astype(BF16)
            return c

        lax.fori_loop(0, D_MODEL // CAST_ROWS, cast, 0)

    @pl.when(used)
    def _():
        words = xb_ref[...]
        lo = lax.bitcast_convert_type(words << 16, F32).astype(BF16)
        hi = lax.bitcast_convert_type(words & jnp.uint32(0xFFFF0000), F32).astype(BF16)
        half = D_MODEL // 2
        h = _dot(lo, wup_ref[:half, :]) + _dot(hi, wup_ref[half:, :]) + bup_ref[0]
        gate = jnp.minimum(h[:, :D_FF], SWIGLU_LIMIT)
        up = jnp.clip(h[:, D_FF:], -SWIGLU_LIMIT, SWIGLU_LIMIT)
        act = (up + 1.0) * (gate * jax.nn.sigmoid(SWIGLU_ALPHA * gate))
        yb_ref[...] = _dot(act.astype(BF16), wdn_ref[...]) + bdn_ref[0]


def _expert_call(blk_e, n_used, blk_first, blk_slot, blk_next, xb, wup, bup, wdn, bdn):
    n_rows = xb.shape[0]
    d = D_MODEL
    assert D_FF == d
    rows = EXPERT_ROWS
    last = lambda r, be, nu, *_: jnp.maximum(jnp.minimum(r, nu[0] - 1), 0)
    exp3 = lambda r, be, nu, *_: (be[last(r, be, nu)], 0, 0)
    return pl.pallas_call(
        _expert_kernel,
        grid_spec=pltpu.PrefetchScalarGridSpec(
            num_scalar_prefetch=5,
            grid=(n_rows // rows,),
            in_specs=[pl.BlockSpec((rows, d // 2), lambda r, be, nu, *_: (last(r, be, nu), 0)),
                      pl.BlockSpec((1, 1, 2 * D_FF), exp3),
                      pl.BlockSpec((1, 1, d), exp3),
                      pl.BlockSpec(memory_space=pl.ANY),
                      pl.BlockSpec(memory_space=pl.ANY)],
            out_specs=pl.BlockSpec((rows, d), lambda r, *_: (r, 0)),
            scratch_shapes=[pltpu.VMEM((2, d, 2 * D_FF), F32),
                            pltpu.VMEM((2, D_FF, d), F32),
                            pltpu.VMEM((d, 2 * D_FF), BF16),
                            pltpu.VMEM((D_FF, d), BF16),
                            pltpu.SemaphoreType.DMA((2, 2))]),
        out_shape=jax.ShapeDtypeStruct((n_rows, d), F32),
        compiler_params=_params(1),
        name="experts",
    )(blk_e, n_used, blk_first, blk_slot, blk_next, xb, bup, bdn, wup, wdn)


def _combine_kernel(seg_ref, seg_next_ref, x1_ref, pos_ref, post_ref, gates_ref, mod_ref, g_ref, b_ref,
                    yb_ref, o_ref, ybuf_ref, sems):
    tc = x1_ref.shape[0]
    n_sorted = SORTED_ROWS
    step = pl.program_id(0)
    slot = step % 2

    def copies_into(s):
        def make_copy(src, dst, size):
            return pltpu.make_async_copy(yb_ref.at[pl.ds(dst, size)],
                                         ybuf_ref.at[s, pl.ds(src, size)], sems.at[s])
        return make_copy

    def fetch(tile_seg_ref, s):
        ybuf_ref[s, TOP_K * tc:, :] = jnp.zeros((n_sorted - TOP_K * tc, ybuf_ref.shape[2]), F32)
        _segment_copies(tile_seg_ref, copies_into(s), lambda cp: cp.start())

    @pl.when(step == 0)
    def _():
        fetch(seg_ref, slot)

    @pl.when(step + 1 < pl.num_programs(0))
    def _():
        fetch(seg_next_ref, 1 - slot)

    pos = pos_ref[...]
    gates = gates_ref[...]
    grid = lax.broadcasted_iota(jnp.int32, (n_sorted, tc), 0)
    gsel = 0.0
    for k in range(TOP_K):
        gsel = jnp.where(grid == pos[k:k + 1, :], gates[k:k + 1, :], gsel)
    gate_col = jnp.sum(gsel, axis=1, keepdims=True)
    unsort = _sort_matrix(post_ref[...], n_sorted, transposed=True)

    _segment_copies(seg_ref, copies_into(slot), lambda cp: cp.wait())

    yg = ybuf_ref[slot] * gate_col
    y_hi = yg.astype(BF16)
    y_lo = (yg - y_hi.astype(F32)).astype(BF16)
    h2 = _dot(unsort, y_hi) + _dot(unsort, y_lo)
    g2 = mod_ref[0, 5:6, :]
    o_ref[...] = _ln_plain(DEEPNORM_ALPHA * x1_ref[...] + g2 * h2) * g_ref[...] + b_ref[...]


def _combine_call(seg, x1_flat, pos, pos_t, gates, mod, g, b_, yb, seq_len):
    n_tok, d = x1_flat.shape
    tc = MOE_TILE
    c2 = lambda i: (0, 0)
    n_tiles = n_tok // tc
    return pl.pallas_call(
        _combine_kernel,
        grid=(n_tiles,),
        in_specs=[pl.BlockSpec((1, N_EXPERTS, 3), lambda i: (i, 0, 0), memory_space=pltpu.SMEM),
                  pl.BlockSpec((1, N_EXPERTS, 3), lambda i: (jnp.minimum(i + 1, n_tiles - 1), 0, 0),
                               memory_space=pltpu.SMEM),
                  pl.BlockSpec((tc, d), lambda i: (i, 0)),
                  pl.BlockSpec((TOP_K, tc), lambda i: (0, i)),
                  pl.BlockSpec((tc, TOP_K), lambda i: (i, 0)),
                  pl.BlockSpec((TOP_K, tc), lambda i: (0, i)),
                  pl.BlockSpec((1, 6, d), lambda i: ((i * tc) // seq_len, 0, 0)),
                  pl.BlockSpec((1, d), c2),
                  pl.BlockSpec((1, d), c2),
                  pl.BlockSpec(memory_space=pl.ANY)],
        out_specs=pl.BlockSpec((tc, d), lambda i: (i, 0)),
        out_shape=jax.ShapeDtypeStruct((n_tok, d), F32),
        scratch_shapes=[pltpu.VMEM((2, SORTED_ROWS, d), F32), pltpu.SemaphoreType.DMA((2,))],
        compiler_params=_params(1),
        name="combine",
    )(seg, seg, x1_flat, pos, pos_t, gates, mod, g, b_, yb)


def _rope_tables(seq_len):
    inv_freq = ROPE_THETA ** (-jnp.arange(0, HEAD_DIM, 2, dtype=F32) / HEAD_DIM)
    ang = jnp.arange(seq_len, dtype=F32)[:, None] * inv_freq[None, :]
    cos, sin = lax.optimization_barrier((jnp.cos(ang), jnp.sin(ang)))
    zero = jnp.zeros_like(sin)
    reps = HEAD_W // HEAD_DIM
    cos_t = jnp.tile(jnp.concatenate([cos, cos], axis=1), (1, reps))
    sina_t = jnp.tile(jnp.concatenate([-sin, zero], axis=1), (1, reps))
    sinb_t = jnp.tile(jnp.concatenate([zero, sin], axis=1), (1, reps))
    return cos_t, sina_t, sinb_t


def _encode(x, mod, w):
    b, s, d = x.shape
    n_tok = b * s
    cos, sina, sinb = w["rope"]
    ua, q, k, vt, sga, sgb = _inproj_call(x, mod, cos, sina, sinb, w["wcat"], w["wvt"])
    yattn = _attn_call(q, k, vt, w["lq1"], w["lk1"], w["lq2"], w["lk2"], w["subw"])
    x1, pos, gates, seg, cnt = _mixer_call(
        x, ua, sga, sgb, yattn, mod, w["poolw"], w["pscale"], w["wa"], w["wb"], w["wout"],
        w["ln1_g"], w["ln1_b"], w["rwt"], w["rb"])

    rows = EXPERT_ROWS
    counts = cnt[:, 0].astype(jnp.int32)
    padded = (counts + rows - 1) // rows * rows
    pends = jnp.cumsum(padded)
    pstarts = pends - padded
    seg = seg.at[:, :, 2].add(pstarts[None, :])
    n_tiles = n_tok // MOE_TILE
    n_blocks = (n_tiles * SORTED_ROWS) // rows + N_EXPERTS
    blk_row0 = jnp.arange(n_blocks, dtype=jnp.int32) * rows
    blk_e = jnp.minimum(jnp.sum((pends[None, :] <= blk_row0[:, None]).astype(jnp.int32), axis=1),
                        N_EXPERTS - 1)
    n_used = (pends[-1:] // rows).astype(jnp.int32)
    blk_ids = jnp.arange(n_blocks, dtype=jnp.int32)
    blk_first = ((blk_ids == 0) | (blk_e != jnp.roll(blk_e, 1))).astype(jnp.int32)
    blk_slot = (jnp.cumsum(blk_first) - 1) % 2
    e_ids = jnp.arange(N_EXPERTS, dtype=jnp.int32)
    later = (e_ids[None, :] > e_ids[:, None]) & (padded[None, :] > 0)
    next_present = jnp.min(jnp.where(later, e_ids[None, :], N_EXPERTS), axis=1)
    next_present = jnp.where(next_present == N_EXPERTS, -1, next_present)
    blk_next = jnp.sum(jnp.where(blk_e[:, None] == e_ids[None, :], next_present[None, :], 0), axis=1)

    x1_flat = x1.reshape(n_tok, d)
    xb = _dispatch_call(pends.astype(jnp.int32), padded, n_used, seg, x1_flat, pos, mod, n_blocks * rows, s)
    yb = _expert_call(blk_e, n_used, blk_first, blk_slot.astype(jnp.int32), blk_next.astype(jnp.int32),
                      xb, w["wup"], w["bup"], w["wdn"], w["bdn"])
    out = _combine_call(seg, x1_flat, pos, pos.T, gates, mod, w["ln2_g"], w["ln2_b"], yb, s)
    return out.reshape(b, s, d)


def kernel(x_prompt, x_sample, c_prompt, c_sample, w_ada, b_ada, w_in, pool_w, pool_scale, w_branch_a, w_branch_b, lambda_q1, lambda_k1, lambda_q2, lambda_k2, subln_w, w_out, ln1_g, ln1_b, router_w, router_b, exp_w_up, exp_b_up, exp_w_down, exp_b_down, ln2_g, ln2_b):
    d = D_MODEL
    w_in0 = w_in[0]
    seg = lambda j: w_in0[:, j * d:(j + 1) * d]
    w = {
        "wcat": jnp.concatenate([seg(0), seg(1), seg(2), seg(4), seg(5)], axis=1).astype(BF16),
        "wvt": seg(3).T.astype(BF16),
        "lq1": lambda_q1, "lk1": lambda_k1, "lq2": lambda_q2, "lk2": lambda_k2,
        "subw": subln_w[0].reshape(HEAD_W, 1),
        "poolw": pool_w[0].astype(BF16),
        "pscale": pool_scale,
        "wa": w_branch_a[0].astype(BF16),
        "wb": w_branch_b[0].astype(BF16),
        "wout": w_out[0].astype(BF16),
        "ln1_g": ln1_g, "ln1_b": ln1_b,
        "rwt": router_w[0].T.astype(BF16),
        "rb": router_b[0].reshape(N_EXPERTS, 1),
        "wup": exp_w_up[0],
        "bup": exp_b_up[0].reshape(N_EXPERTS, 1, 2 * D_FF),
        "wdn": exp_w_down[0],
        "bdn": exp_b_down[0].reshape(N_EXPERTS, 1, d),
        "ln2_g": ln2_g, "ln2_b": ln2_b,
        "rope": _rope_tables(max(x_prompt.shape[1], x_sample.shape[1])),
    }
    nb_p, nb_s = c_prompt.shape[0], c_sample.shape[0]
    pad = (-(nb_p + nb_s)) % 8
    c_all = jnp.concatenate([c_prompt, c_sample, jnp.zeros((pad, d), F32)], axis=0)
    mod = _mod_call(c_all, w_ada[0], b_ada[0]).reshape(-1, 6, d)
    y_prompt = _encode(x_prompt, mod[:nb_p], w)
    y_sample = _encode(x_sample, mod[nb_p:nb_p + nb_s], w)
    return (y_prompt, y_sample)
```
